```python
import math
import jax, jax.numpy as jnp
from jax import lax
import numpy as np

D_MODEL = 2048
BATCH = 4
SEQ = 8192
DEPTH = 1

HEAD_DIM = 128
N_HEADS_MOBA = D_MODEL // (2 * HEAD_DIM)
N_HEADS_NSA = D_MODEL // (2 * HEAD_DIM)
NSA_KV_HEADS = 2
N_HEADS_TOTAL = N_HEADS_MOBA + N_HEADS_NSA
MOBA_BLOCK = 256
MOBA_TOPK = 3
MOBA_Q_CHUNK = 16
CMP_LEN = 32
CMP_STRIDE = 16
CMP_HIDDEN = HEAD_DIM
SLC_BLOCK = 64
SLC_TOPK = 16
WINDOW = 512
NSA_Q_CHUNK = 32
N_EXPERTS = 64
N_GROUPS = 8
TOPK_GROUPS = 4
TOP_K = 8
D_EXPERT = 512
ROUTED_SCALE = 2.5
EXPERT_CHUNK = 256
PLE_DIM = 256
RMS_EPS = 1e-6

COL_SPLITS = [
    N_HEADS_MOBA * HEAD_DIM, N_HEADS_MOBA * HEAD_DIM, N_HEADS_MOBA * HEAD_DIM,
    N_HEADS_NSA * HEAD_DIM,
    NSA_KV_HEADS * HEAD_DIM, NSA_KV_HEADS * HEAD_DIM,
    NSA_KV_HEADS * HEAD_DIM, NSA_KV_HEADS * HEAD_DIM,
    NSA_KV_HEADS * HEAD_DIM, NSA_KV_HEADS * HEAD_DIM,
    N_HEADS_NSA * 3,
]
D_IN = sum(COL_SPLITS)

kernel_name = "hybrid_moba_nsa_moe_ple_block"


def rms_norm(x, g):
    xf = x.astype(jnp.float32)
    y = xf * lax.rsqrt(jnp.mean(xf * xf, axis=-1, keepdims=True) + RMS_EPS)
    return (y * g.astype(jnp.float32)).astype(x.dtype)


def masked_softmax(s, mask):
    s = jnp.where(mask, s.astype(jnp.float32), -jnp.inf)
    m = jnp.max(s, axis=-1, keepdims=True)
    m = jnp.where(jnp.isfinite(m), m, 0.0)
    e = jnp.where(mask, jnp.exp(s - m), 0.0)
    return e / jnp.maximum(jnp.sum(e, axis=-1, keepdims=True), 1e-30)


def alibi_slopes(n):
    return jnp.asarray(2.0 ** (-8.0 * np.arange(1, n + 1) / n), dtype=jnp.float32)


def moba_attention(q, k, v, slopes):
    B, T, H, d = q.shape
    nb = -(-T // MOBA_BLOCK)
    Tp = nb * MOBA_BLOCK
    pad = ((0, 0), (0, Tp - T), (0, 0), (0, 0))
    q, k, v = [jnp.pad(a, pad).transpose(0, 2, 1, 3) for a in (q, k, v)]
    kb = k.reshape(B, H, nb, MOBA_BLOCK, d)
    vb = v.reshape(B, H, nb, MOBA_BLOCK, d)
    kmean = jnp.mean(kb.astype(jnp.float32), axis=3)
    scale = d ** -0.5
    slope = slopes[None, :, None, None]
    bi = jnp.arange(B)[:, None, None, None]
    hi = jnp.arange(H)[None, :, None, None]
    n_sel = min(MOBA_TOPK, nb)
    QC = MOBA_Q_CHUNK
    blk_pos = jnp.arange(MOBA_BLOCK)

    def chunk(c):
        t0 = c * QC
        t = t0 + jnp.arange(QC)
        bt = t0 // MOBA_BLOCK
        qc = lax.dynamic_slice_in_dim(q, t0, QC, axis=2).astype(jnp.float32)
        gate = jnp.einsum('bhqd,bhnd->bhqn', qc, kmean)
        gate = jnp.where(jnp.arange(nb) < bt, gate, -jnp.inf)
        gsc, idx = lax.top_k(gate, n_sel)
        sel_valid = gsc > -jnp.inf
        k_sel = kb[bi, hi, idx].astype(jnp.float32)
        v_sel = vb[bi, hi, idx].reshape(B, H, QC, n_sel * MOBA_BLOCK, d)
        s_sel = jnp.einsum('bhqd,bhqnkd->bhqnk', qc, k_sel) * scale
        pos_sel = idx[..., None] * MOBA_BLOCK + blk_pos
        s_sel = s_sel - slope[..., None] * (t[:, None, None] - pos_sel)
        s_sel = s_sel.reshape(B, H, QC, n_sel * MOBA_BLOCK)
        m_sel = jnp.broadcast_to(sel_valid[..., None], (B, H, QC, n_sel, MOBA_BLOCK))
        m_sel = m_sel.reshape(B, H, QC, n_sel * MOBA_BLOCK)
        k_own = lax.dynamic_slice_in_dim(k, bt * MOBA_BLOCK, MOBA_BLOCK, axis=2).astype(jnp.float32)
        v_own = lax.dynamic_slice_in_dim(v, bt * MOBA_BLOCK, MOBA_BLOCK, axis=2)
        pos_own = bt * MOBA_BLOCK + blk_pos
        s_own = jnp.einsum('bhqd,bhkd->bhqk', qc, k_own) * scale
        s_own = s_own - slope * (t[:, None] - pos_own[None, :])
        m_own = jnp.broadcast_to(pos_own[None, :] <= t[:, None], (B, H, QC, MOBA_BLOCK))
        probs = masked_softmax(jnp.concatenate([s_sel, s_own], axis=-1),
                               jnp.concatenate([m_sel, m_own], axis=-1))
        ns = n_sel * MOBA_BLOCK
        o = (jnp.einsum('bhqk,bhqkd->bhqd', probs[..., :ns], v_sel)
             + jnp.einsum('bhqk,bhkd->bhqd', probs[..., ns:], v_own))
        return o.astype(q.dtype)

    out = lax.map(chunk, jnp.arange(Tp // QC))
    return out.transpose(1, 0, 3, 2, 4).reshape(B, Tp, H, d)[:, :T]


def compress_tokens(a, pos, w1, w2):
    B, T, G, d = a.shape
    nc = (T - CMP_LEN) // CMP_STRIDE + 1
    idx = jnp.arange(nc)[:, None] * CMP_STRIDE + jnp.arange(CMP_LEN)[None, :]
    blk = a[:, idx] + pos[None, None, :, None, :]
    flat = blk.transpose(0, 1, 3, 2, 4).reshape(B, nc, G, CMP_LEN * d)
    hid = jax.nn.gelu(flat @ w1)
    return (hid @ w2).transpose(0, 2, 1, 3)


def nsa_attention(q, kc, vc, ks, vs, kw, vw, gates, slopes,
                  pos_k, w1_k, w2_k, pos_v, w1_v, w2_v):
    B, T, H, d = q.shape
    G = kc.shape[2]
    R = H // G
    scale = d ** -0.5
    QC = NSA_Q_CHUNK
    k_cmp = compress_tokens(kc, pos_k, w1_k, w2_k).astype(jnp.float32)
    v_cmp = compress_tokens(vc, pos_v, w1_v, w2_v)
    nc = k_cmp.shape[2]
    cstart = jnp.arange(nc) * CMP_STRIDE
    cend = cstart + CMP_LEN - 1
    ns = T // SLC_BLOCK
    n_slc = min(SLC_TOPK, ns)
    sstart = jnp.arange(ns) * SLC_BLOCK
    overlap = ((cend[:, None] >= sstart[None, :]) &
               (cstart[:, None] <= sstart[None, :] + SLC_BLOCK - 1)).astype(jnp.float32)
    ks_blk = ks.transpose(0, 2, 1, 3).reshape(B, G, ns, SLC_BLOCK, d)
    vs_blk = vs.transpose(0, 2, 1, 3).reshape(B, G, ns, SLC_BLOCK, d)
    kw_p = jnp.pad(kw.transpose(0, 2, 1, 3), ((0, 0), (0, 0), (WINDOW, 0), (0, 0)))
    vw_p = jnp.pad(vw.transpose(0, 2, 1, 3), ((0, 0), (0, 0), (WINDOW, 0), (0, 0)))
    qg = q.reshape(B, T, G, R, d).transpose(0, 2, 3, 1, 4)
    gg = gates.reshape(B, T, G, R, 3).transpose(0, 2, 3, 1, 4)
    slope = slopes.reshape(G, R)[None, :, :, None, None]
    bi = jnp.arange(B)[:, None, None, None]
    gi = jnp.arange(G)[None, :, None, None]
    blk_id = jnp.arange(ns)
    slc_pos = jnp.arange(SLC_BLOCK)
    win_off = jnp.arange(WINDOW + QC)

    def chunk(c):
        t0 = c * QC
        t = t0 + jnp.arange(QC)
        qc = lax.dynamic_slice_in_dim(qg, t0, QC, axis=3).astype(jnp.float32)
        s_c = jnp.einsum('bgrqd,bgnd->bgrqn', qc, k_cmp) * scale
        s_c = s_c - slope * (t[:, None] - cend[None, :])
        p_c = masked_softmax(s_c, cend[None, :] <= t[:, None])
        o_c = jnp.einsum('bgrqn,bgnd->bgrqd', p_c, v_cmp)
        imp = jnp.einsum('bgrqn,ns->bgqs', p_c, overlap)
        bt = t // SLC_BLOCK
        forced = ((blk_id[None, :] == 0) | (blk_id[None, :] == bt[:, None]) |
                  (blk_id[None, :] == bt[:, None] - 1))
        pri = jnp.where(blk_id[None, :] > bt[:, None], -jnp.inf,
                        jnp.where(forced, jnp.inf, imp))
        psc, idx = lax.top_k(pri, n_slc)
        valid = psc > -jnp.inf
        k_sel = ks_blk[bi, gi, idx].astype(jnp.float32)
        v_sel = vs_blk[bi, gi, idx].reshape(B, G, QC, n_slc * SLC_BLOCK, d)
        s_s = jnp.einsum('bgrqd,bgqnkd->bgrqnk', qc, k_sel) * scale
        pos_s = idx[..., None] * SLC_BLOCK + slc_pos
        dist_s = (t[:, None, None] - pos_s)[:, :, None]
        s_s = s_s - slope[..., None] * dist_s
        m_s = (valid[..., None] & (pos_s <= t[:, None, None]))[:, :, None]
        m_s = jnp.broadcast_to(m_s, s_s.shape).reshape(B, G, R, QC, n_slc * SLC_BLOCK)
        p_s = masked_softmax(s_s.reshape(B, G, R, QC, n_slc * SLC_BLOCK), m_s)
        o_s = jnp.einsum('bgrqk,bgqkd->bgrqd', p_s, v_sel)
        kwc = lax.dynamic_slice_in_dim(kw_p, t0, WINDOW + QC, axis=2).astype(jnp.float32)
        vwc = lax.dynamic_slice_in_dim(vw_p, t0, WINDOW + QC, axis=2)
        pos_w = t0 - WINDOW + win_off
        s_w = jnp.einsum('bgrqd,bgkd->bgrqk', qc, kwc) * scale
        s_w = s_w - slope * (t[:, None] - pos_w[None, :])
        m_w = ((pos_w[None, :] >= 0) & (pos_w[None, :] <= t[:, None]) &
               (t[:, None] - pos_w[None, :] < WINDOW))
        p_w = masked_softmax(s_w, m_w)
        o_w = jnp.einsum('bgrqk,bgkd->bgrqd', p_w, vwc)
        g = lax.dynamic_slice_in_dim(gg, t0, QC, axis=3).astype(jnp.float32)
        o = g[..., 0:1] * o_c + g[..., 1:2] * o_s + g[..., 2:3] * o_w
        return o.astype(q.dtype)

    out = lax.map(chunk, jnp.arange(T // QC))
    return out.transpose(1, 0, 4, 2, 3, 5).reshape(B, T, H, d)


def swiglu(h, wg, wu, wd):
    return (jax.nn.silu(h @ wg) * (h @ wu)) @ wd


def moe_ffn(h, w_router, router_bias, w_gate_e, w_up_e, w_down_e, w_gate_s, w_up_s, w_down_s):
    B, T, D = h.shape
    N = B * T
    hf = h.reshape(N, D)
    aff = jax.nn.sigmoid((hf @ w_router).astype(jnp.float32))
    biased = aff + router_bias.astype(jnp.float32)
    gscore = jnp.sum(lax.top_k(biased.reshape(N, N_GROUPS, N_EXPERTS // N_GROUPS), 2)[0], axis=-1)
    _, gidx = lax.top_k(gscore, TOPK_GROUPS)
    gmask = jnp.any(gidx[..., None] == jnp.arange(N_GROUPS), axis=-2)
    emask = jnp.repeat(gmask, N_EXPERTS // N_GROUPS, axis=1)
    _, eidx = lax.top_k(jnp.where(emask, biased, -jnp.inf), TOP_K)
    w = jnp.take_along_axis(aff, eidx, axis=1)
    w = w / jnp.sum(w, axis=-1, keepdims=True) * ROUTED_SCALE
    NK = N * TOP_K
    C = EXPERT_CHUNK
    flat_e = eidx.reshape(NK)
    flat_tok = jnp.arange(NK, dtype=jnp.int32) // TOP_K
    flat_w = w.reshape(NK)
    order = jnp.argsort(flat_e)
    se, stok, sw = flat_e[order], flat_tok[order], flat_w[order]
    counts = jnp.zeros((N_EXPERTS,), jnp.int32).at[flat_e].add(1)
    start = jnp.cumsum(counts) - counts
    pcounts = (counts + C - 1) // C * C
    pend = jnp.cumsum(pcounts)
    pstart = pend - pcounts
    dest = pstart[se] + (jnp.arange(NK, dtype=jnp.int32) - start[se])
    n_chunks = -(-(NK + N_EXPERTS * (C - 1)) // C)
    P = n_chunks * C
    buf_tok = jnp.full((P,), N, jnp.int32).at[dest].set(stok)
    buf_w = jnp.zeros((P,), jnp.float32).at[dest].set(sw)
    chunk_e = jnp.minimum(jnp.searchsorted(pend, jnp.arange(n_chunks, dtype=jnp.int32) * C,
                                           side='right'), N_EXPERTS - 1)
    h_pad = jnp.concatenate([hf, jnp.zeros((1, D), hf.dtype)], axis=0)

    def expert_chunk(acc, c):
        tok = lax.dynamic_slice_in_dim(buf_tok, c * C, C)
        wt = lax.dynamic_slice_in_dim(buf_w, c * C, C)
        e = chunk_e[c]
        y = swiglu(h_pad[tok], w_gate_e[e], w_up_e[e], w_down_e[e])
        return acc.at[tok].add(y.astype(jnp.float32) * wt[:, None]), None

    acc, _ = lax.scan(expert_chunk, jnp.zeros((N + 1, D), jnp.float32), jnp.arange(n_chunks))
    routed = acc[:N].astype(h.dtype)
    shared = swiglu(hf, w_gate_s, w_up_s, w_down_s)
    return (routed + shared).reshape(B, T, D)


def setup_inputs(seed: int = 0) -> dict:
    key = jax.random.key(seed)
    ks = jax.random.split(key, 32)
    f32 = jnp.float32
    L, D, d, G = DEPTH, D_MODEL, HEAD_DIM, NSA_KV_HEADS
    D_MOBA = N_HEADS_MOBA * HEAD_DIM
    D_NSA = N_HEADS_NSA * HEAD_DIM

    def nrm(k, shape, scale):
        return jax.random.normal(k, shape, f32) * scale

    def gain(k, shape):
        return 1.0 + 0.02 * jax.random.normal(k, shape, f32)

    return {
        "x": nrm(ks[0], (BATCH, SEQ, D), 1.0),
        "p": nrm(ks[1], (L, BATCH, SEQ, PLE_DIM), 1.0),
        "norm_mix_g": gain(ks[2], (L, D)),
        "w_in": nrm(ks[3], (L, D, D_IN), D ** -0.5),
        "moba_out_g": gain(ks[4], (L, D_MOBA)),
        "nsa_out_g": gain(ks[5], (L, D_NSA)),
        "cmp_pos_k": nrm(ks[6], (L, CMP_LEN, d), 0.02),
        "cmp_w1_k": nrm(ks[7], (L, CMP_LEN * d, CMP_HIDDEN), (CMP_LEN * d) ** -0.5),
        "cmp_w2_k": nrm(ks[8], (L, CMP_HIDDEN, d), CMP_HIDDEN ** -0.5),
        "cmp_pos_v": nrm(ks[9], (L, CMP_LEN, d), 0.02),
        "cmp_w1_v": nrm(ks[10], (L, CMP_LEN * d, CMP_HIDDEN), (CMP_LEN * d) ** -0.5),
        "cmp_w2_v": nrm(ks[11], (L, CMP_HIDDEN, d), CMP_HIDDEN ** -0.5),
        "w_out": nrm(ks[12], (L, D_MOBA + D_NSA, D), (D_MOBA + D_NSA) ** -0.5),
        "norm_ffn_g": gain(ks[13], (L, D)),
        "w_router": nrm(ks[14], (L, D, N_EXPERTS), D ** -0.5),
        "router_bias": nrm(ks[15], (L, N_EXPERTS), 0.01),
        "w_gate_e": nrm(ks[16], (L, N_EXPERTS, D, D_EXPERT), D ** -0.5),
        "w_up_e": nrm(ks[17], (L, N_EXPERTS, D, D_EXPERT), D ** -0.5),
        "w_down_e": nrm(ks[18], (L, N_EXPERTS, D_EXPERT, D), D_EXPERT ** -0.5),
        "w_gate_s": nrm(ks[19], (L, D, D_EXPERT), D ** -0.5),
        "w_up_s": nrm(ks[20], (L, D, D_EXPERT), D ** -0.5),
        "w_down_s": nrm(ks[21], (L, D_EXPERT, D), D_EXPERT ** -0.5),
        "norm_ple_g": gain(ks[22], (L, D)),
        "w_ple": nrm(ks[23], (L, PLE_DIM, D), PLE_DIM ** -0.5),
        "w_ple_gate": nrm(ks[24], (L, D, D), D ** -0.5),
        "norm_final_g": gain(ks[25], (D,)),
    }


def reference(x, p, norm_mix_g, w_in, moba_out_g, nsa_out_g,
              cmp_pos_k, cmp_w1_k, cmp_w2_k, cmp_pos_v, cmp_w1_v, cmp_w2_v,
              w_out, norm_ffn_g, w_router, router_bias,
              w_gate_e, w_up_e, w_down_e, w_gate_s, w_up_s, w_down_s,
              norm_ple_g, w_ple, w_ple_gate, norm_final_g):
    B, T, _ = x.shape
    slopes = alibi_slopes(N_HEADS_TOTAL)
    moba_slopes = slopes[0::2]
    nsa_slopes = slopes[1::2]
    split_at = [int(s) for s in np.cumsum(COL_SPLITS)[:-1]]
    for i in range(DEPTH):
        h = rms_norm(x, norm_mix_g[i])
        proj = h @ w_in[i]
        mq, mk, mv, nq, kc, vc, ksl, vsl, kwn, vwn, ng = jnp.split(proj, split_at, axis=-1)
        hm = lambda a: a.reshape(B, T, N_HEADS_MOBA, HEAD_DIM)
        hk = lambda a: a.reshape(B, T, NSA_KV_HEADS, HEAD_DIM)
        o_m = moba_attention(hm(mq), hm(mk), hm(mv), moba_slopes)
        gates = jax.nn.sigmoid(ng.astype(jnp.float32)).reshape(B, T, N_HEADS_NSA, 3)
        o_n = nsa_attention(nq.reshape(B, T, N_HEADS_NSA, HEAD_DIM),
                            hk(kc), hk(vc), hk(ksl), hk(vsl), hk(kwn), hk(vwn),
                            gates, nsa_slopes,
                            cmp_pos_k[i], cmp_w1_k[i], cmp_w2_k[i],
                            cmp_pos_v[i], cmp_w1_v[i], cmp_w2_v[i])
        o_m = rms_norm(o_m.reshape(B, T, -1), moba_out_g[i])
        o_n = rms_norm(o_n.reshape(B, T, -1), nsa_out_g[i])
        x = x + jnp.concatenate([o_m, o_n], axis=-1) @ w_out[i]
        x = x + moe_ffn(rms_norm(x, norm_ffn_g[i]), w_router[i], router_bias[i],
                        w_gate_e[i], w_up_e[i], w_down_e[i],
                        w_gate_s[i], w_up_s[i], w_down_s[i])
        gate = jax.nn.sigmoid(rms_norm(x, norm_ple_g[i]) @ w_ple_gate[i])
        x = x + (p[i] @ w_ple[i]) * gate
    return rms_norm(x, norm_final_g)
```

```python
import functools

import numpy as np
import jax
import jax.numpy as jnp
from jax import lax
from jax.experimental import pallas as pl
from jax.experimental.pallas import tpu as pltpu

F32 = jnp.float32
BF16 = jnp.bfloat16
I32 = jnp.int32

D_MODEL = 2048
HEAD_DIM = 128
N_HEADS_MOBA = 8
N_HEADS_NSA = 8
NSA_KV_HEADS = 2
NSA_REP = N_HEADS_NSA // NSA_KV_HEADS
MOBA_BLOCK = 256
MOBA_TOPK = 3
CMP_LEN = 32
CMP_STRIDE = 16
SLC_BLOCK = 64
SLC_TOPK = 16
WINDOW = 512
N_EXPERTS = 64
N_GROUPS = 8
GROUP_SIZE = N_EXPERTS // N_GROUPS
TOPK_GROUPS = 4
TOP_K = 8
D_EXPERT = 512
ROUTED_SCALE = 2.5
PLE_DIM = 256
RMS_EPS = 1e-6
D_MOBA = N_HEADS_MOBA * HEAD_DIM
D_NSA = N_HEADS_NSA * HEAD_DIM
D_KV = NSA_KV_HEADS * HEAD_DIM
COL_MQ, COL_MK, COL_MV, COL_NQ = 0, D_MOBA, 2 * D_MOBA, 3 * D_MOBA
COL_KC = COL_NQ + D_NSA
COL_VC, COL_KS, COL_VS, COL_KW, COL_VW = (COL_KC + D_KV * i for i in range(1, 6))
D_PROJ = COL_VW + D_KV
COL_GATE = D_PROJ

LANES = 128
V7X_VMEM_LIMIT = 56 * 1024 * 1024
NEG = -1e30

EXPERT_TILE = 256
SCALE = HEAD_DIM ** -0.5
NT = (((1,), (1,)), ((), ()))


def _cparams(sem):
    return pltpu.CompilerParams(dimension_semantics=sem, vmem_limit_bytes=V7X_VMEM_LIMIT)


def _dot(a, b):
    return jnp.dot(a, b, preferred_element_type=F32)


def _dot_nt(a, b):
    return lax.dot_general(a, b, NT, preferred_element_type=F32)


def _split2(a):
    hi = a.astype(BF16)
    lo = (a - hi.astype(F32)).astype(BF16)
    return hi, lo


def _rms(x, g):
    ms = jnp.mean(x * x, axis=-1, keepdims=True)
    return x * lax.rsqrt(ms + RMS_EPS) * g


def _sigmoid(x):
    return 1.0 / (1.0 + jnp.exp(-x))


def _softmax_update(s, v, m, l, acc):
    m_new = jnp.maximum(m, jnp.max(s, axis=1, keepdims=True))
    alpha = jnp.exp(m - m_new)
    p = jnp.exp(s - m_new)
    l = alpha * l + jnp.sum(p, axis=1, keepdims=True)
    acc = alpha * acc + _dot(p.astype(BF16), v)
    return m_new, l, acc


def _pick_topk(score, lane, k, sentinel):
    sel = jnp.zeros(score.shape, jnp.bool_)
    g = score
    for _ in range(k):
        m = jnp.max(g, axis=1, keepdims=True)
        idx = jnp.min(jnp.where(g == m, lane, sentinel), axis=1, keepdims=True)
        idx = jnp.where(m > -jnp.inf, idx, sentinel)
        pick = lane == idx
        sel = sel | pick
        g = jnp.where(pick, -jnp.inf, g)
    return sel


def _inproj_kernel(x_ref, g_ref, w_ref, wg_ref, proj_ref, gate_ref, h_scr):
    @pl.when(pl.program_id(1) == 0)
    def _():
        hb = _rms(x_ref[...], g_ref[...]).astype(BF16)
        h_scr[...] = hb
        gate_ref[...] = _dot(hb, wg_ref[...])

    proj_ref[...] = _dot(h_scr[...], w_ref[...]).astype(BF16)


def _inproj(x2, g, w_main, w_gate):
    n = x2.shape[0]
    tm = min(512, n)
    tn = 512
    return pl.pallas_call(
        _inproj_kernel,
        grid=(n // tm, D_PROJ // tn),
        in_specs=[
            pl.BlockSpec((tm, D_MODEL), lambda i, j: (i, 0)),
            pl.BlockSpec((1, D_MODEL), lambda i, j: (0, 0)),
            pl.BlockSpec((D_MODEL, tn), lambda i, j: (0, j)),
            pl.BlockSpec((D_MODEL, 2 * LANES), lambda i, j: (0, 0)),
        ],
        out_specs=[
            pl.BlockSpec((tm, tn), lambda i, j: (i, j)),
            pl.BlockSpec((tm, 2 * LANES), lambda i, j: (i, 0)),
        ],
        out_shape=[jax.ShapeDtypeStruct((n, D_PROJ), BF16), jax.ShapeDtypeStruct((n, 2 * LANES), F32)],
        scratch_shapes=[pltpu.VMEM((tm, D_MODEL), BF16)],
        compiler_params=_cparams(("parallel", "arbitrary")),
        name="inproj",
    )(x2, g, w_main, w_gate)


def _kmean_kernel(k_ref, o_ref):
    k = k_ref[0].astype(F32)
    o_ref[0] = jnp.mean(k.reshape(8, MOBA_BLOCK, D_MOBA), axis=1)


def _kmean(proj3):
    b, t, _ = proj3.shape
    nb = t // MOBA_BLOCK
    return pl.pallas_call(
        _kmean_kernel,
        grid=(b, nb // 8),
        in_specs=[pl.BlockSpec((1, 8 * MOBA_BLOCK, D_MOBA), lambda bi, j: (bi, j, COL_MK // D_MOBA))],
        out_specs=pl.BlockSpec((1, 8, D_MOBA), lambda bi, j: (bi, j, 0)),
        out_shape=jax.ShapeDtypeStruct((b, nb, D_MOBA), F32),
        compiler_params=_cparams(("parallel", "parallel")),
        name="moba_kmean",
    )(proj3)


def _moba_kernel(slopes_ref, q_ref, k_ref, v_ref, km_ref, o_ref):
    h = pl.program_id(1)
    i = pl.program_id(2)
    slope = slopes_ref[h]
    blk = MOBA_BLOCK
    q = q_ref[0]
    km_hi, km_lo = _split2(km_ref[0])
    gate = _dot_nt(q, km_hi) + _dot_nt(q, km_lo)
    lane = lax.broadcasted_iota(I32, gate.shape, 1)
    sel = _pick_topk(jnp.where(lane < i, gate, -jnp.inf), lane, MOBA_TOPK, LANES)
    sel = sel | (lane == i)
    qx = jnp.concatenate([q, jnp.where(sel, 0.0, NEG).astype(BF16)], axis=1)
    row = lax.broadcasted_iota(I32, (blk, blk), 0)
    col = lax.broadcasted_iota(I32, (blk, blk), 1)
    cr = (col - row).astype(F32)
    klane = lax.broadcasted_iota(I32, (blk, LANES), 1)

    def body(n, carry):
        m, l, acc = carry
        start = pl.multiple_of(n * blk, blk)
        k = k_ref[0, pl.ds(start, blk), :]
        v = v_ref[0, pl.ds(start, blk), :]
        kx = jnp.concatenate([k, jnp.where(klane == n, 1.0, 0.0).astype(BF16)], axis=1)
        rel = cr + ((n - i) * blk).astype(F32)
        s = _dot_nt(qx, kx) * SCALE + slope * rel
        s = jnp.where(rel <= 0.0, s, NEG)
        return _softmax_update(s, v, m, l, acc)

    init = (jnp.full((blk, 1), NEG, F32), jnp.zeros((blk, 1), F32), jnp.zeros((blk, HEAD_DIM), F32))
    m, l, acc = lax.fori_loop(0, i + 1, body, init)
    o_ref[0] = acc / l


def _moba(proj3, kmean_pad, slopes):
    b, t, _ = proj3.shape
    nb = t // MOBA_BLOCK
    qb, kb, vb = COL_MQ // HEAD_DIM, COL_MK // HEAD_DIM, COL_MV // HEAD_DIM
    return pl.pallas_call(
        _moba_kernel,
        grid_spec=pltpu.PrefetchScalarGridSpec(
            num_scalar_prefetch=1,
            grid=(b, N_HEADS_MOBA, nb),
            in_specs=[
                pl.BlockSpec((1, MOBA_BLOCK, HEAD_DIM), lambda bi, h, i, s: (bi, i, qb + h)),
                pl.BlockSpec((1, t, HEAD_DIM), lambda bi, h, i, s: (bi, 0, kb + h)),
                pl.BlockSpec((1, t, HEAD_DIM), lambda bi, h, i, s: (bi, 0, vb + h)),
                pl.BlockSpec((1, LANES, HEAD_DIM), lambda bi, h, i, s: (bi, 0, h)),
            ],
            out_specs=pl.BlockSpec((1, MOBA_BLOCK, HEAD_DIM), lambda bi, h, i, s: (bi, i, h)),
        ),
        out_shape=jax.ShapeDtypeStruct((b, t, D_MOBA), F32),
        compiler_params=_cparams(("parallel", "parallel", "arbitrary")),
        name="moba_attn",
    )(slopes, proj3, proj3, proj3, kmean_pad)


def _gelu_tanh(x):
    c = np.float32(np.sqrt(2.0 / np.pi))
    return x * (0.5 * (1.0 + jnp.tanh(c * (x + 0.044715 * (x * x * x)))))


def _cmp_kernel(x_ref, w1c_ref, w1_ref, pos_ref, w2_ref, o_ref):
    x = x_ref[0, 0, 0]
    nch = x.shape[0]
    ab = _dot(x, w1c_ref[0])
    first = ab[:, :HEAD_DIM]
    second = pltpu.roll(ab[:, HEAD_DIM:], shift=nch - 1, axis=0)
    p_hi, p_lo = _split2(pos_ref[0])
    w_hi, w_lo = _split2(w1_ref[0])
    posc = _dot(p_hi, w_hi) + _dot(p_hi, w_lo) + _dot(p_lo, w_hi)
    hid = _gelu_tanh(first + second + posc[0:1])
    o_ref[0, 0, 0] = _dot(hid.astype(BF16), w2_ref[0]).astype(BF16)


def _compress(xc, w1cat, w1, pos, w2):
    _, b, g, nch, _ = xc.shape
    hid = HEAD_DIM
    return pl.pallas_call(
        _cmp_kernel,
        grid=(2, b, g),
        in_specs=[
            pl.BlockSpec((1, 1, 1, nch, CMP_STRIDE * HEAD_DIM), lambda s, bi, gi: (s, bi, gi, 0, 0)),
            pl.BlockSpec((1, CMP_STRIDE * HEAD_DIM, 2 * hid), lambda s, bi, gi: (s, 0, 0)),
            pl.BlockSpec((1, CMP_LEN * HEAD_DIM, hid), lambda s, bi, gi: (s, 0, 0)),
            pl.BlockSpec((1, 8, CMP_LEN * HEAD_DIM), lambda s, bi, gi: (s, 0, 0)),
            pl.BlockSpec((1, hid, HEAD_DIM), lambda s, bi, gi: (s, 0, 0)),
        ],
        out_specs=pl.BlockSpec((1, 1, 1, nch, HEAD_DIM), lambda s, bi, gi: (s, bi, gi, 0, 0)),
        out_shape=jax.ShapeDtypeStruct((2, b, g, nch, HEAD_DIM), BF16),
        compiler_params=_cparams(("parallel", "parallel", "parallel")),
        name="nsa_compress",
    )(xc, w1cat, w1, pos, w2)


CMP_TQ = 256


def _nsa_cmp_kernel(n_slc, slopes_ref, q_ref, kc_ref, vc_ref, ov_ref, oc_ref, sb_ref):
    g = pl.program_id(1)
    i = pl.program_id(2)
    tq = CMP_TQ
    kc = kc_ref[0, 0, 0]
    vc = vc_ref[0, 0, 0]
    nch = kc.shape[0]
    t = i * tq + lax.broadcasted_iota(I32, (tq, nch), 0)
    cend = lax.broadcasted_iota(I32, (tq, nch), 1) * CMP_STRIDE + (CMP_LEN - 1)
    valid = cend <= t
    dist = (t - cend).astype(F32)
    psum = jnp.zeros((tq, nch), F32)
    for r in range(NSA_REP):
        slope = slopes_ref[g * NSA_REP + r]
        q = q_ref[0, :, r * HEAD_DIM:(r + 1) * HEAD_DIM]
        s = jnp.where(valid, _dot_nt(q, kc) * SCALE - slope * dist, NEG)
        m = jnp.max(s, axis=1, keepdims=True)
        m = jnp.where(m > 0.5 * NEG, m, 0.0)
        e = jnp.where(valid, jnp.exp(s - m), 0.0)
        p = e / jnp.maximum(jnp.sum(e, axis=1, keepdims=True), 1e-30)
        oc_ref[0, :, r * HEAD_DIM:(r + 1) * HEAD_DIM] = _dot(p.astype(BF16), vc)
        psum = psum + p
    p_hi = psum.astype(BF16)
    rem = psum - p_hi.astype(F32)
    p_mid = rem.astype(BF16)
    p_lo = (rem - p_mid.astype(F32)).astype(BF16)
    ov = ov_ref[...]
    imp = _dot(p_hi, ov) + _dot(p_mid, ov) + _dot(p_lo, ov)
    lane = lax.broadcasted_iota(I32, (tq, LANES), 1)
    bt = (i * tq + lax.broadcasted_iota(I32, (tq, LANES), 0)) // SLC_BLOCK
    forced = (lane == 0) | (lane == bt) | (lane == bt - 1)
    pri = jnp.where(lane > bt, -jnp.inf, jnp.where(forced, jnp.inf, imp))
    sel = _pick_topk(pri, lane, n_slc, LANES)
    sb_ref[0, 0] = jnp.where(sel, 0.0, NEG).astype(BF16)


def _nsa_cmp(proj3, kvc, overlap, slopes):
    b, t, _ = proj3.shape
    nch = kvc.shape[3]
    n_slc = min(SLC_TOPK, t // SLC_BLOCK)
    qblk = COL_NQ // (NSA_REP * HEAD_DIM)
    gw = NSA_REP * HEAD_DIM
    return pl.pallas_call(
        functools.partial(_nsa_cmp_kernel, n_slc),
        grid_spec=pltpu.PrefetchScalarGridSpec(
            num_scalar_prefetch=1,
            grid=(b, NSA_KV_HEADS, t // CMP_TQ),
            in_specs=[
                pl.BlockSpec((1, CMP_TQ, gw), lambda bi, g, i, s: (bi, i, qblk + g)),
                pl.BlockSpec((1, 1, 1, nch, HEAD_DIM), lambda bi, g, i, s: (0, bi, g, 0, 0)),
                pl.BlockSpec((1, 1, 1, nch, HEAD_DIM), lambda bi, g, i, s: (1, bi, g, 0, 0)),
                pl.BlockSpec((nch, LANES), lambda bi, g, i, s: (0, 0)),
            ],
            out_specs=[
                pl.BlockSpec((1, CMP_TQ, gw), lambda bi, g, i, s: (bi, i, g)),
                pl.BlockSpec((1, 1, CMP_TQ, LANES), lambda bi, g, i, s: (bi, g, i, 0)),
            ],
        ),
        out_shape=[jax.ShapeDtypeStruct((b, t, D_NSA), F32),
                   jax.ShapeDtypeStruct((b, NSA_KV_HEADS, t, LANES), BF16)],
        compiler_params=_cparams(("parallel", "parallel", "arbitrary")),
        name="nsa_cmp_attn",
    )(slopes, proj3, kvc, kvc, overlap)


NSA_TQ = 128
SEL_TK = 256
WIN_TK = 128


def _nsa_main_kernel(slopes_ref, q_ref, ks_ref, vs_ref, kw_ref, vw_ref, sb_ref, oc_ref, gt_ref, o_ref):
    g = pl.program_id(1)
    i = pl.program_id(2)
    tq, rows = NSA_TQ, NSA_REP * NSA_TQ
    t0 = i * tq
    qs = jnp.concatenate([q_ref[0, :, r * HEAD_DIM:(r + 1) * HEAD_DIM] for r in range(NSA_REP)], axis=0)
    sb = sb_ref[0, 0]
    qx = jnp.concatenate([qs, jnp.concatenate([sb] * NSA_REP, axis=0)], axis=1)
    slope_col = jnp.concatenate(
        [jnp.full((tq, 1), slopes_ref[g * NSA_REP + r], F32) for r in range(NSA_REP)], axis=0)
    init = (jnp.full((rows, 1), NEG, F32), jnp.zeros((rows, 1), F32), jnp.zeros((rows, HEAD_DIM), F32))

    def rel_pos(start, tk):
        rq = lax.broadcasted_iota(I32, (rows, tk), 0) % tq
        ck = lax.broadcasted_iota(I32, (rows, tk), 1)
        return (ck - rq + (start - t0)).astype(F32)

    def sel_body(n, carry):
        start = pl.multiple_of(n * SEL_TK, SEL_TK)
        k = ks_ref[0, pl.ds(start, SEL_TK), :]
        v = vs_ref[0, pl.ds(start, SEL_TK), :]
        kblk = (start + lax.broadcasted_iota(I32, (SEL_TK, LANES), 0)) // SLC_BLOCK
        onehot = jnp.where(lax.broadcasted_iota(I32, (SEL_TK, LANES), 1) == kblk, 1.0, 0.0).astype(BF16)
        rel = rel_pos(start, SEL_TK)
        s = _dot_nt(qx, jnp.concatenate([k, onehot], axis=1)) * SCALE + slope_col * rel
        s = jnp.where(rel <= 0.0, s, NEG)
        return _softmax_update(s, v, *carry)

    _, l_s, acc_s = lax.fori_loop(0, (t0 + tq - 1) // SEL_TK + 1, sel_body, init)

    def win_body(n, carry):
        start = pl.multiple_of(n * WIN_TK, WIN_TK)
        k = kw_ref[0, pl.ds(start, WIN_TK), :]
        v = vw_ref[0, pl.ds(start, WIN_TK), :]
        rel = rel_pos(start, WIN_TK)
        s = _dot_nt(qs, k) * SCALE + slope_col * rel
        s = jnp.where(rel <= 0.0, jnp.where(rel > -float(WINDOW), s, NEG), NEG)
        return _softmax_update(s, v, *carry)

    first_tile = jnp.maximum(i * (tq // WIN_TK) - WINDOW // WIN_TK, 0)
    _, l_w, acc_w = lax.fori_loop(first_tile, (i + 1) * (tq // WIN_TK), win_body, init)

    o_s = acc_s / l_s
    o_w = acc_w / l_w
    gt = _sigmoid(gt_ref[0])
    for r in range(NSA_REP):
        dsl = slice(r * HEAD_DIM, (r + 1) * HEAD_DIM)
        rsl = slice(r * tq, (r + 1) * tq)
        o_ref[0, :, dsl] = (gt[:, 3 * r:3 * r + 1] * oc_ref[0, :, dsl]
                            + gt[:, 3 * r + 1:3 * r + 2] * o_s[rsl]
                            + gt[:, 3 * r + 2:3 * r + 3] * o_w[rsl])


def _nsa_main(proj3, selbias, o_c, gates3, slopes):
    b, t, _ = proj3.shape
    gw = NSA_REP * HEAD_DIM
    qblk = COL_NQ // gw
    cb = lambda col: col // HEAD_DIM
    kv_spec = lambda col: pl.BlockSpec((1, t, HEAD_DIM), lambda bi, g, i, s: (bi, 0, cb(col) + g))
    return pl.pallas_call(
        _nsa_main_kernel,
        grid_spec=pltpu.PrefetchScalarGridSpec(
            num_scalar_prefetch=1,
            grid=(b, NSA_KV_HEADS, t // NSA_TQ),
            in_specs=[
                pl.BlockSpec((1, NSA_TQ, gw), lambda bi, g, i, s: (bi, i, qblk + g)),
                kv_spec(COL_KS), kv_spec(COL_VS), kv_spec(COL_KW), kv_spec(COL_VW),
                pl.BlockSpec((1, 1, NSA_TQ, LANES), lambda bi, g, i, s: (bi, g, i, 0)),
                pl.BlockSpec((1, NSA_TQ, gw), lambda bi, g, i, s: (bi, i, g)),
                pl.BlockSpec((1, NSA_TQ, LANES), lambda bi, g, i, s: (bi, i, g)),
            ],
            out_specs=pl.BlockSpec((1, NSA_TQ, gw), lambda bi, g, i, s: (bi, i, g)),
        ),
        out_shape=jax.ShapeDtypeStruct((b, t, D_NSA), F32),
        compiler_params=_cparams(("parallel", "parallel", "arbitrary")),
        name="nsa_main",
    )(slopes, proj3, proj3, proj3, proj3, proj3, selbias, o_c, gates3)


def _outproj_kernel(om_ref, on_ref, x_ref, gm_ref, gn_ref, wt_ref, wb_ref, o_ref):
    a = _rms(om_ref[...], gm_ref[...]).astype(BF16)
    c = _rms(on_ref[...], gn_ref[...]).astype(BF16)
    o_ref[...] = x_ref[...] + (_dot(a, wt_ref[...]) + _dot(c, wb_ref[...]))


def _outproj(o_m, o_n, x2, gm, gn, w_top, w_bot):
    n = x2.shape[0]
    tm = min(512, n)
    row = lambda w: pl.BlockSpec((tm, w), lambda i: (i, 0))
    const = lambda r, c: pl.BlockSpec((r, c), lambda i: (0, 0))
    return pl.pallas_call(
        _outproj_kernel,
        grid=(n // tm,),
        in_specs=[row(D_MOBA), row(D_NSA), row(D_MODEL), const(1, D_MOBA), const(1, D_NSA),
                  const(D_MOBA, D_MODEL), const(D_NSA, D_MODEL)],
        out_specs=row(D_MODEL),
        out_shape=jax.ShapeDtypeStruct((n, D_MODEL), F32),
        compiler_params=_cparams(("parallel",)),
        name="outproj",
    )(o_m, o_n, x2, gm, gn, w_top, w_bot)


ROUTER_TM = 512


def _router_kernel(x_ref, g_ref, wr_ref, bias_ref, tri_ref, h_ref, eidx_ref, wgt_ref, pos_ref, cnt_ref, carry):
    step = pl.program_id(0)
    tm = ROUTER_TM

    @pl.when(step == 0)
    def _():
        carry[...] = jnp.zeros_like(carry)

    h = _rms(x_ref[...], g_ref[...])
    h_ref[...] = h
    h_hi, h_lo = _split2(h)
    w_hi, w_lo = _split2(wr_ref[...])
    logits = _dot_nt(w_hi, h_hi) + _dot_nt(w_hi, h_lo) + _dot_nt(w_lo, h_hi)
    aff = _sigmoid(logits)
    biased = aff + bias_ref[...][:, 0:1]

    b3 = biased.reshape(N_GROUPS, GROUP_SIZE, tm)
    sub = lax.broadcasted_iota(I32, b3.shape, 1)
    m1 = jnp.max(b3, axis=1, keepdims=True)
    i1 = jnp.min(jnp.where(b3 == m1, sub, GROUP_SIZE), axis=1, keepdims=True)
    m2 = jnp.max(jnp.where(sub == i1, -jnp.inf, b3), axis=1, keepdims=True)
    gscore = (m1 + m2).reshape(N_GROUPS, tm)

    giota = lax.broadcasted_iota(I32, (N_GROUPS, tm), 0)
    gsel = jnp.zeros((N_GROUPS, tm), jnp.bool_)
    gs = gscore
    for _ in range(TOPK_GROUPS):
        m = jnp.max(gs, axis=0, keepdims=True)
        idx = jnp.min(jnp.where(gs == m, giota, N_GROUPS), axis=0, keepdims=True)
        pick = giota == idx
        gsel = gsel | pick
        gs = jnp.where(pick, -jnp.inf, gs)
    emask = jnp.broadcast_to(jnp.where(gsel, 1.0, 0.0)[:, None, :], (N_GROUPS, GROUP_SIZE, tm)).reshape(N_EXPERTS, tm)

    eiota = lax.broadcasted_iota(I32, (N_EXPERTS, tm), 0)
    cand = jnp.where(emask > 0.5, biased, -jnp.inf)
    multi = jnp.zeros((N_EXPERTS, tm), F32)
    idxs, wts = [], []
    for _ in range(TOP_K):
        m = jnp.max(cand, axis=0, keepdims=True)
        idx = jnp.min(jnp.where(cand == m, eiota, N_EXPERTS), axis=0, keepdims=True)
        pick = eiota == idx
        idxs.append(idx)
        wts.append(jnp.sum(jnp.where(pick, aff, 0.0), axis=0, keepdims=True))
        multi = jnp.where(pick, 1.0, multi)
        cand = jnp.where(pick, -jnp.inf, cand)
    wsum = wts[0]
    for w in wts[1:]:
        wsum = wsum + w

    base = carry[...][:, 0:1]
    cum = _dot(multi.astype(BF16), tri_ref[...]) + base
    poss = [jnp.sum(jnp.where(eiota == idx, cum, 0.0), axis=0, keepdims=True) for idx in idxs]
    total = base + jnp.sum(multi, axis=1, keepdims=True)
    carry[...] = jnp.broadcast_to(total, carry.shape)
    cnt_ref[...] = jnp.broadcast_to(total, cnt_ref.shape)
    eidx_ref[...] = jnp.concatenate(idxs, axis=0)
    wgt_ref[...] = jnp.concatenate([w / wsum * ROUTED_SCALE for w in wts], axis=0)
    pos_ref[...] = jnp.concatenate(poss, axis=0).astype(I32)


def _router(x1, g, wr_t, bias, tri):
    n = x1.shape[0]
    tm = ROUTER_TM
    const = lambda r, c: pl.BlockSpec((r, c), lambda i: (0, 0))
    tok = lambda: pl.BlockSpec((TOP_K, tm), lambda i: (0, i))
    return pl.pallas_call(
        _router_kernel,
        grid=(n // tm,),
        in_specs=[pl.BlockSpec((tm, D_MODEL), lambda i: (i, 0)), const(1, D_MODEL), const(N_EXPERTS, D_MODEL),
                  const(N_EXPERTS, LANES), const(tm, tm)],
        out_specs=[pl.BlockSpec((tm, D_MODEL), lambda i: (i, 0)), tok(), tok(), tok(), const(N_EXPERTS, LANES)],
        out_shape=[jax.ShapeDtypeStruct((n, D_MODEL), F32), jax.ShapeDtypeStruct((TOP_K, n), I32),
                   jax.ShapeDtypeStruct((TOP_K, n), F32), jax.ShapeDtypeStruct((TOP_K, n), I32),
                   jax.ShapeDtypeStruct((N_EXPERTS, LANES), F32)],
        scratch_shapes=[pltpu.VMEM((N_EXPERTS, LANES), F32)],
        compiler_params=_cparams(("arbitrary",)),
        name="router",
    )(x1, g, wr_t, bias, tri)


DISPATCH_TM = 256
ZERO_ROWS = EXPERT_TILE + 8


def _dispatch_kernel(pad_ref, tail_ref, dest_ref, h_hbm, xs_hbm, zeros_vmem, sem, zsem):
    step = pl.program_id(0)
    tm = DISPATCH_TM

    @pl.when(step == 0)
    def _():
        zeros_vmem[...] = jnp.zeros_like(zeros_vmem)

        def fill(e, _):
            start = pl.multiple_of((pad_ref[e] // 8) * 8, 8)
            pltpu.make_async_copy(zeros_vmem, xs_hbm.at[pl.ds(start, ZERO_ROWS), :], zsem).start()
            return 0

        lax.fori_loop(0, N_EXPERTS, fill, 0)

        def drain(e, _):
            pltpu.make_async_copy(zeros_vmem, xs_hbm.at[pl.ds(0, ZERO_ROWS), :], zsem).wait()
            return 0

        lax.fori_loop(0, N_EXPERTS, drain, 0)

        tail_start = tail_ref[0]
        n_tail = (xs_hbm.shape[0] - tail_start) // EXPERT_TILE

        def fill_tail(i, _):
            start = pl.multiple_of(tail_start + i * EXPERT_TILE, EXPERT_TILE)
            pltpu.make_async_copy(zeros_vmem.at[pl.ds(0, EXPERT_TILE), :],
                                  xs_hbm.at[pl.ds(start, EXPERT_TILE), :], zsem).start()
            return 0

        lax.fori_loop(0, n_tail, fill_tail, 0)

        def drain_tail(i, _):
            pltpu.make_async_copy(zeros_vmem.at[pl.ds(0, EXPERT_TILE), :],
                                  xs_hbm.at[pl.ds(0, EXPERT_TILE), :], zsem).wait()
            return 0

        lax.fori_loop(0, n_tail, drain_tail, 0)

    base = step * tm

    def issue(j, _):
        for k in range(TOP_K):
            pltpu.make_async_copy(h_hbm.at[pl.ds(base + j, 1), :],
                                  xs_hbm.at[pl.ds(dest_ref[j * TOP_K + k], 1), :], sem).start()
        return 0

    lax.fori_loop(0, tm, issue, 0)

    n_rows = tm * TOP_K
    pltpu.make_async_copy(h_hbm.at[pl.ds(0, n_rows), :], xs_hbm.at[pl.ds(0, n_rows), :], sem).wait()


def _dispatch(h2, dest_flat, pad_start, tail_start, p_alloc):
    n = h2.shape[0]
    tm = min(DISPATCH_TM, n)
    return pl.pallas_call(
        _dispatch_kernel,
        grid_spec=pltpu.PrefetchScalarGridSpec(
            num_scalar_prefetch=2,
            grid=(n // tm,),
            in_specs=[pl.BlockSpec((tm * TOP_K,), lambda i, s, u: (i,), memory_space=pltpu.SMEM),
                      pl.BlockSpec(memory_space=pl.ANY)],
            out_specs=pl.BlockSpec(memory_space=pl.ANY),
            scratch_shapes=[pltpu.VMEM((ZERO_ROWS, D_MODEL), F32), pltpu.SemaphoreType.DMA,
                            pltpu.SemaphoreType.DMA],
        ),
        out_shape=jax.ShapeDtypeStruct((p_alloc, D_MODEL), F32),
        compiler_params=_cparams(("arbitrary",)),
        name="moe_dispatch",
    )(pad_start, tail_start, dest_flat, h2)


def _moe_kernel(ce_ref, nu_ref, x_ref, wg_ref, wu_ref, wd_ref, y_ref, wg_b, wu_b, wd_b):
    c = pl.program_id(0)

    @pl.when(c < nu_ref[0])
    def _():
        @pl.when((c == 0) | (ce_ref[c] != ce_ref[jnp.maximum(c - 1, 0)]))
        def _():
            wg_b[...] = wg_ref[0].astype(BF16)
            wu_b[...] = wu_ref[0].astype(BF16)
            wd_b[...] = wd_ref[0].astype(BF16)

        x = x_ref[...].astype(BF16)
        gp = _dot(x, wg_b[...])
        up = _dot(x, wu_b[...])
        y_ref[...] = _dot((gp * _sigmoid(gp) * up).astype(BF16), wd_b[...])

    @pl.when(c >= nu_ref[0])
    def _():
        y_ref[...] = jnp.zeros_like(y_ref)


def _moe(xs, chunk_e, n_used, w_gate_e, w_up_e, w_down_e, n_chunks):
    c = EXPERT_TILE
    row = lambda i, ce, nu: (jnp.minimum(i, nu[0] - 1), 0)
    wsel = lambda i, ce, nu: (ce[jnp.minimum(i, nu[0] - 1)], 0, 0)
    return pl.pallas_call(
        _moe_kernel,
        grid_spec=pltpu.PrefetchScalarGridSpec(
            num_scalar_prefetch=2,
            grid=(n_chunks,),
            in_specs=[pl.BlockSpec((c, D_MODEL), row),
                      pl.BlockSpec((1, D_MODEL, D_EXPERT), wsel),
                      pl.BlockSpec((1, D_MODEL, D_EXPERT), wsel),
                      pl.BlockSpec((1, D_EXPERT, D_MODEL), wsel)],
            out_specs=pl.BlockSpec((c, D_MODEL), lambda i, ce, nu: (i, 0)),
            scratch_shapes=[pltpu.VMEM((D_MODEL, D_EXPERT), BF16), pltpu.VMEM((D_MODEL, D_EXPERT), BF16),
                            pltpu.VMEM((D_EXPERT, D_MODEL), BF16)],
        ),
        out_shape=jax.ShapeDtypeStruct((n_chunks * c, D_MODEL), F32),
        compiler_params=_cparams(("arbitrary",)),
        name="moe_experts",
    )(chunk_e, n_used, xs, w_gate_e, w_up_e, w_down_e)


COMBINE_TM = 64


def _combine_kernel(dest_ref, w_ref, y_hbm, o_ref, rows, sem):
    tm = COMBINE_TM

    def issue(j, _):
        for k in range(TOP_K):
            pltpu.make_async_copy(y_hbm.at[pl.ds(dest_ref[j * TOP_K + k], 1), :],
                                  rows.at[pl.ds(k * tm + j, 1), :], sem).start()
        return 0

    lax.fori_loop(0, tm, issue, 0)

    pltpu.make_async_copy(y_hbm.at[pl.ds(0, tm * TOP_K), :], rows, sem).wait()
    w = w_ref[...]
    acc = w[:, 0:1] * rows[0:tm, :]
    for k in range(1, TOP_K):
        acc = acc + w[:, k:k + 1] * rows[k * tm:(k + 1) * tm, :]
    o_ref[...] = acc


def _combine(ys, dest_flat, w_tok):
    n = w_tok.shape[0]
    tm = COMBINE_TM
    return pl.pallas_call(
        _combine_kernel,
        grid=(n // tm,),
        in_specs=[pl.BlockSpec((tm * TOP_K,), lambda i: (i,), memory_space=pltpu.SMEM),
                  pl.BlockSpec((tm, TOP_K), lambda i: (i, 0)),
                  pl.BlockSpec(memory_space=pl.ANY)],
        out_specs=pl.BlockSpec((tm, D_MODEL), lambda i: (i, 0)),
        out_shape=jax.ShapeDtypeStruct((n, D_MODEL), F32),
        scratch_shapes=[pltpu.VMEM((tm * TOP_K, D_MODEL), F32), pltpu.SemaphoreType.DMA],
        compiler_params=_cparams(("arbitrary",)),
        name="moe_combine",
    )(dest_flat, w_tok, ys)


def _final_kernel(last, x1_ref, rt_ref, h_ref, p_ref, wg_ref, wu_ref, wd_ref, gp_ref, wple_ref, wpg_ref, gf_ref, o_ref):
    hb = h_ref[...].astype(BF16)
    gp = _dot(hb, wg_ref[...])
    up = _dot(hb, wu_ref[...])
    shared = _dot((gp * _sigmoid(gp) * up).astype(BF16), wd_ref[...])
    x2 = x1_ref[...] + (rt_ref[...] + shared)
    gate = _sigmoid(_dot(_rms(x2, gp_ref[...]).astype(BF16), wpg_ref[...]))
    x3 = x2 + _dot(p_ref[...].astype(BF16), wple_ref[...]) * gate
    o_ref[...] = _rms(x3, gf_ref[...]) if last else x3


def _final(x1, routed, h2, p2, wgs, wus, wds, g_ple, w_ple, w_pg, g_final, last):
    n = x1.shape[0]
    tm = min(128, n)
    row = lambda w: pl.BlockSpec((tm, w), lambda i: (i, 0))
    const = lambda r, c: pl.BlockSpec((r, c), lambda i: (0, 0))
    return pl.pallas_call(
        functools.partial(_final_kernel, last),
        grid=(n // tm,),
        in_specs=[row(D_MODEL), row(D_MODEL), row(D_MODEL), row(PLE_DIM),
                  const(D_MODEL, D_EXPERT), const(D_MODEL, D_EXPERT), const(D_EXPERT, D_MODEL),
                  const(1, D_MODEL), const(PLE_DIM, D_MODEL), const(D_MODEL, D_MODEL), const(1, D_MODEL)],
        out_specs=row(D_MODEL),
        out_shape=jax.ShapeDtypeStruct((n, D_MODEL), F32),
        compiler_params=_cparams(("parallel",)),
        name="shared_ple_final",
    )(x1, routed, h2, p2, wgs, wus, wds, g_ple, w_ple, w_pg, g_final)


def _alibi_slopes(n):
    return jnp.asarray(2.0 ** (-8.0 * np.arange(1, n + 1) / n), dtype=F32)


def _layer(x, p_i, norm_mix_g, w_in, moba_out_g, nsa_out_g, cmp_pos_k, cmp_w1_k, cmp_w2_k,
           cmp_pos_v, cmp_w1_v, cmp_w2_v, w_out, norm_ffn_g, w_router, router_bias,
           w_gate_e, w_up_e, w_down_e, w_gate_s, w_up_s, w_down_s, norm_ple_g, w_ple, w_ple_gate,
           norm_final_g, last):
    b, t, _ = x.shape
    n = b * t
    slopes = _alibi_slopes(N_HEADS_MOBA + N_HEADS_NSA)
    x2 = x.reshape(n, D_MODEL)

    w_main = w_in[:, :D_PROJ].astype(BF16)
    wg = w_in[:, COL_GATE:].reshape(D_MODEL, NSA_KV_HEADS, NSA_REP * 3)
    w_gate = jnp.pad(wg, ((0, 0), (0, 0), (0, LANES - NSA_REP * 3))).reshape(D_MODEL, 2 * LANES).astype(BF16)
    proj, gates = _inproj(x2, norm_mix_g.reshape(1, D_MODEL), w_main, w_gate)
    proj3 = proj.reshape(b, t, D_PROJ)
    gates3 = gates.reshape(b, t, 2 * LANES)

    nb = t // MOBA_BLOCK
    kmean = _kmean(proj3)
    kmean_pad = jnp.pad(kmean, ((0, 0), (0, LANES - nb), (0, 0)))
    o_m = _moba(proj3, kmean_pad, slopes[0::2])

    nch = t // CMP_STRIDE
    kvc = jnp.stack([proj3[:, :, COL_KC:COL_KC + D_KV], proj3[:, :, COL_VC:COL_VC + D_KV]])
    kvc = kvc.reshape(2, b, t, NSA_KV_HEADS, HEAD_DIM).transpose(0, 1, 3, 2, 4)
    kvc = kvc.reshape(2, b, NSA_KV_HEADS, nch, CMP_STRIDE * HEAD_DIM)
    w1 = jnp.stack([cmp_w1_k, cmp_w1_v])
    half = CMP_STRIDE * HEAD_DIM
    w1cat = jnp.concatenate([w1[:, :half], w1[:, half:]], axis=2).astype(BF16)
    pos = jnp.stack([cmp_pos_k, cmp_pos_v]).reshape(2, 1, CMP_LEN * HEAD_DIM)
    pos = jnp.broadcast_to(pos, (2, 8, CMP_LEN * HEAD_DIM))
    w2 = jnp.stack([cmp_w2_k, cmp_w2_v]).astype(BF16)
    kv_cmp = _compress(kvc, w1cat, w1, pos, w2)
    cs = np.arange(nch)[:, None] * CMP_STRIDE
    ss = np.arange(LANES)[None, :] * SLC_BLOCK
    overlap = jnp.asarray(((cs + CMP_LEN - 1 >= ss) & (cs <= ss + SLC_BLOCK - 1)).astype(np.float32), dtype=BF16)
    o_c, selbias = _nsa_cmp(proj3, kv_cmp, overlap, slopes[1::2])
    o_n = _nsa_main(proj3, selbias, o_c, gates3, slopes[1::2])

    w_out_b = w_out.astype(BF16)
    x1 = _outproj(o_m.reshape(n, D_MOBA), o_n.reshape(n, D_NSA), x2, moba_out_g.reshape(1, D_MOBA),
                  nsa_out_g.reshape(1, D_NSA), w_out_b[:D_MOBA], w_out_b[D_MOBA:])

    tri = jnp.asarray(np.triu(np.ones((ROUTER_TM, ROUTER_TM), np.float32), k=1), dtype=BF16)
    bias = jnp.broadcast_to(router_bias.astype(F32)[:, None], (N_EXPERTS, LANES))
    h2, eidx_t, wgt_t, pos_t, cnt = _router(x1, norm_ffn_g.reshape(1, D_MODEL), w_router.T, bias, tri)
    counts = cnt[:, 0].astype(I32)
    c = EXPERT_TILE
    pcounts = (counts + c - 1) // c * c
    pend = jnp.cumsum(pcounts)
    pstart = pend - pcounts
    dest_flat = (pstart[eidx_t] + pos_t).T.reshape(n * TOP_K)
    n_chunks = -(-(n * TOP_K + N_EXPERTS * (c - 1)) // c)
    n_used = (pend[-1] // c).astype(I32).reshape(1)
    chunk_e = jnp.minimum(jnp.searchsorted(pend, jnp.arange(n_chunks, dtype=I32) * c, side='right'),
                          N_EXPERTS - 1).astype(I32)

    xs = _dispatch(h2, dest_flat, (pstart + counts).astype(I32), pend[-1:].astype(I32), (n_chunks + 2) * c)
    ys = _moe(xs, chunk_e, n_used, w_gate_e, w_up_e, w_down_e, n_chunks)
    routed = _combine(ys, dest_flat, wgt_t.T)

    return _final(x1, routed, h2, p_i.reshape(n, PLE_DIM), w_gate_s.astype(BF16), w_up_s.astype(BF16),
                  w_down_s.astype(BF16), norm_ple_g.reshape(1, D_MODEL), w_ple.astype(BF16),
                  w_ple_gate.astype(BF16), norm_final_g.reshape(1, D_MODEL), last).reshape(b, t, D_MODEL)


def kernel(x, p, norm_mix_g, w_in, moba_out_g, nsa_out_g, cmp_pos_k, cmp_w1_k, cmp_w2_k, cmp_pos_v, cmp_w1_v, cmp_w2_v, w_out, norm_ffn_g, w_router, router_bias, w_gate_e, w_up_e, w_down_e, w_gate_s, w_up_s, w_down_s, norm_ple_g, w_ple, w_ple_gate, norm_final_g):
    per_layer = (norm_mix_g, w_in, moba_out_g, nsa_out_g, cmp_pos_k, cmp_w1_k, cmp_w2_k, cmp_pos_v, cmp_w1_v,
                 cmp_w2_v, w_out, norm_ffn_g, w_router, router_bias, w_gate_e, w_up_e, w_down_e,
                 w_gate_s, w_up_s, w_down_s, norm_ple_g, w_ple, w_ple_gate)
    depth = p.shape[0]
    for i in range(depth):
        x = _layer(x, p[i], *(a[i] for a in per_layer), norm_final_g, i == depth - 1)
    return x
```

```python
import functools

import numpy as np
import jax
import jax.numpy as jnp
from jax import lax
from jax.experimental import pallas as pl
from jax.experimental.pallas import tpu as pltpu

F32 = jnp.float32
BF16 = jnp.bfloat16
I32 = jnp.int32

D_MODEL = 2048
HEAD_DIM = 128
N_HEADS_MOBA = 8
N_HEADS_NSA = 8
NSA_KV_HEADS = 2
NSA_REP = N_HEADS_NSA // NSA_KV_HEADS
MOBA_BLOCK = 256
MOBA_TOPK = 3
CMP_LEN = 32
CMP_STRIDE = 16
SLC_BLOCK = 64
SLC_TOPK = 16
WINDOW = 512
N_EXPERTS = 64
N_GROUPS = 8
GROUP_SIZE = N_EXPERTS // N_GROUPS
TOPK_GROUPS = 4
TOP_K = 8
D_EXPERT = 512
ROUTED_SCALE = 2.5
PLE_DIM = 256
RMS_EPS = 1e-6
D_MOBA = N_HEADS_MOBA * HEAD_DIM
D_NSA = N_HEADS_NSA * HEAD_DIM
D_KV = NSA_KV_HEADS * HEAD_DIM
COL_MQ, COL_MK, COL_MV, COL_NQ = 0, D_MOBA, 2 * D_MOBA, 3 * D_MOBA
COL_KC = COL_NQ + D_NSA
COL_VC, COL_KS, COL_VS, COL_KW, COL_VW = (COL_KC + D_KV * i for i in range(1, 6))
D_PROJ = COL_VW + D_KV
COL_GATE = D_PROJ

LANES = 128
V7X_VMEM_LIMIT = 56 * 1024 * 1024
NEG = -1e30

EXPERT_TILE = 256
SCALE = HEAD_DIM ** -0.5
NT = (((1,), (1,)), ((), ()))


def _cparams(sem):
    return pltpu.CompilerParams(dimension_semantics=sem, vmem_limit_bytes=V7X_VMEM_LIMIT)


def _dot(a, b):
    return jnp.dot(a, b, preferred_element_type=F32)


def _dot_nt(a, b):
    return lax.dot_general(a, b, NT, preferred_element_type=F32)


def _split2(a):
    hi = a.astype(BF16)
    lo = (a - hi.astype(F32)).astype(BF16)
    return hi, lo


def _rms(x, g):
    ms = jnp.mean(x * x, axis=-1, keepdims=True)
    return x * lax.rsqrt(ms + RMS_EPS) * g


def _sigmoid(x):
    return 1.0 / (1.0 + jnp.exp(-x))


EXP2_SCALE = SCALE * float(np.log2(np.e))
SOFTMAX_ROWS = 64


def _softmax_update(s, vx, m, acc):
    m_new = jnp.maximum(m, jnp.max(s, axis=1, keepdims=True))
    alpha = jnp.exp2((m - m_new) * EXP2_SCALE)
    p = jnp.concatenate(
        [jnp.exp2((s[r:r + SOFTMAX_ROWS] - m_new[r:r + SOFTMAX_ROWS]) * EXP2_SCALE).astype(BF16)
         for r in range(0, s.shape[0], SOFTMAX_ROWS)], axis=0)
    return m_new, alpha * acc + _dot(p, vx)


def _causal_flash(logits, values, n_full, last_bias, rows):
    def body(n, carry):
        s, m, acc = carry
        s_next = logits(n + 1)
        m, acc = _softmax_update(s, values(n), m, acc)
        return s_next, m, acc

    init = (logits(0), jnp.full((rows, 1), NEG, F32), jnp.zeros((rows, 2 * HEAD_DIM), F32))
    s, m, acc = lax.fori_loop(0, n_full, body, init)
    return _softmax_update(s + last_bias, values(n_full), m, acc)[1]


def _alibi_cols(slope_raw, lane, first):
    hi = slope_raw.astype(BF16).astype(F32)
    lo = (slope_raw - hi).astype(BF16).astype(F32)
    is_hi = (lane == first) | (lane == first + 2)
    is_lo = (lane == first + 1) | (lane == first + 3)
    return jnp.where(is_hi, hi, jnp.where(is_lo, lo, 0.0))


def _key_features(t, block, n_lanes_onehot, first):
    pos = np.arange(t)
    f = np.zeros((t, LANES), np.float32)
    f[pos, (pos // block) % n_lanes_onehot] = 1.0
    f[:, first] = f[:, first + 1] = 256 * (pos // 256)
    f[:, first + 2] = f[:, first + 3] = pos % 256
    return jnp.asarray(f, dtype=BF16)


def _pick_topk(score, lane, k, sentinel):
    sel = jnp.zeros(score.shape, jnp.bool_)
    g = score
    for _ in range(k):
        m = jnp.max(g, axis=1, keepdims=True)
        idx = jnp.min(jnp.where(g == m, lane, sentinel), axis=1, keepdims=True)
        idx = jnp.where(m > -jnp.inf, idx, sentinel)
        pick = lane == idx
        sel = sel | pick
        g = jnp.where(pick, -jnp.inf, g)
    return sel


def _inproj_kernel(x_ref, g_ref, w_ref, wg_ref, proj_ref, gate_ref, h_scr):
    @pl.when(pl.program_id(1) == 0)
    def _():
        hb = _rms(x_ref[...], g_ref[...]).astype(BF16)
        h_scr[...] = hb
        gate_ref[...] = _dot(hb, wg_ref[...])

    proj_ref[...] = _dot(h_scr[...], w_ref[...]).astype(BF16)


def _inproj(x2, g, w_main, w_gate):
    n = x2.shape[0]
    tm = min(512, n)
    tn = 512
    return pl.pallas_call(
        _inproj_kernel,
        grid=(n // tm, D_PROJ // tn),
        in_specs=[
            pl.BlockSpec((tm, D_MODEL), lambda i, j: (i, 0)),
            pl.BlockSpec((1, D_MODEL), lambda i, j: (0, 0)),
            pl.BlockSpec((D_MODEL, tn), lambda i, j: (0, j)),
            pl.BlockSpec((D_MODEL, 2 * LANES), lambda i, j: (0, 0)),
        ],
        out_specs=[
            pl.BlockSpec((tm, tn), lambda i, j: (i, j)),
            pl.BlockSpec((tm, 2 * LANES), lambda i, j: (i, 0)),
        ],
        out_shape=[jax.ShapeDtypeStruct((n, D_PROJ), BF16), jax.ShapeDtypeStruct((n, 2 * LANES), F32)],
        scratch_shapes=[pltpu.VMEM((tm, D_MODEL), BF16)],
        compiler_params=_cparams(("parallel", "arbitrary")),
        name="inproj",
    )(x2, g, w_main, w_gate)


def _kmean_kernel(k_ref, o_ref):
    k = k_ref[0].astype(F32)
    o_ref[0] = jnp.mean(k.reshape(8, MOBA_BLOCK, D_MOBA), axis=1)


def _kmean(proj3):
    b, t, _ = proj3.shape
    nb = t // MOBA_BLOCK
    return pl.pallas_call(
        _kmean_kernel,
        grid=(b, nb // 8),
        in_specs=[pl.BlockSpec((1, 8 * MOBA_BLOCK, D_MOBA), lambda bi, j: (bi, j, COL_MK // D_MOBA))],
        out_specs=pl.BlockSpec((1, 8, D_MOBA), lambda bi, j: (bi, j, 0)),
        out_shape=jax.ShapeDtypeStruct((b, nb, D_MOBA), F32),
        compiler_params=_cparams(("parallel", "parallel")),
        name="moba_kmean",
    )(proj3)


MOBA_TQ = 2 * MOBA_BLOCK
MOBA_TK = 2 * MOBA_BLOCK
MOBA_ALIBI_LANE = 120


def _moba_kernel(slopes_ref, q_ref, k_ref, v_ref, km_ref, fk_ref, causal_ref, o_ref):
    h = pl.program_id(1)
    j = pl.program_id(2)
    tq, tk = MOBA_TQ, MOBA_TK
    q = q_ref[0]
    km_hi, km_lo = _split2(km_ref[0])
    gate = _dot_nt(q, km_hi) + _dot_nt(q, km_lo)
    lane = lax.broadcasted_iota(I32, (tq, LANES), 1)
    own = j * (tq // MOBA_BLOCK) + lax.broadcasted_iota(I32, (tq, LANES), 0) // MOBA_BLOCK
    sel = _pick_topk(jnp.where(lane < own, gate, -jnp.inf), lane, MOBA_TOPK, LANES)
    sel = sel | (lane == own)
    slope_raw = jnp.full((tq, LANES), slopes_ref[h] / SCALE, F32)
    extra = jnp.where(lane < MOBA_ALIBI_LANE, jnp.where(sel, 0.0, NEG),
                      _alibi_cols(slope_raw, lane, MOBA_ALIBI_LANE))
    qx = jnp.concatenate([q, extra.astype(BF16)], axis=1)
    ones = jnp.ones((tk, HEAD_DIM), BF16)

    def logits(n):
        start = pl.multiple_of(n * tk, tk)
        kx = jnp.concatenate([k_ref[0, pl.ds(start, tk), :], fk_ref[pl.ds(start, tk), :]], axis=1)
        return _dot_nt(qx, kx)

    def values(n):
        start = pl.multiple_of(n * tk, tk)
        return jnp.concatenate([v_ref[0, pl.ds(start, tk), :], ones], axis=1)

    acc = _causal_flash(logits, values, j, causal_ref[...], tq)
    o_ref[0] = acc[:, :HEAD_DIM] / acc[:, HEAD_DIM:HEAD_DIM + 1]


def _moba(proj3, kmean_pad, slopes):
    b, t, _ = proj3.shape
    assert t // MOBA_BLOCK <= MOBA_ALIBI_LANE and t % MOBA_TQ == 0
    qb, kb, vb = COL_MQ // HEAD_DIM, COL_MK // HEAD_DIM, COL_MV // HEAD_DIM
    fkey = _key_features(t, MOBA_BLOCK, MOBA_ALIBI_LANE, MOBA_ALIBI_LANE)
    causal = jnp.asarray(np.where(np.arange(MOBA_TK)[None, :] <= np.arange(MOBA_TQ)[:, None], 0.0, NEG), dtype=F32)
    return pl.pallas_call(
        _moba_kernel,
        grid_spec=pltpu.PrefetchScalarGridSpec(
            num_scalar_prefetch=1,
            grid=(b, N_HEADS_MOBA, t // MOBA_TQ),
            in_specs=[
                pl.BlockSpec((1, MOBA_TQ, HEAD_DIM), lambda bi, h, j, s: (bi, j, qb + h)),
                pl.BlockSpec((1, t, HEAD_DIM), lambda bi, h, j, s: (bi, 0, kb + h)),
                pl.BlockSpec((1, t, HEAD_DIM), lambda bi, h, j, s: (bi, 0, vb + h)),
                pl.BlockSpec((1, LANES, HEAD_DIM), lambda bi, h, j, s: (bi, 0, h)),
                pl.BlockSpec((t, LANES), lambda bi, h, j, s: (0, 0)),
                pl.BlockSpec((MOBA_TQ, MOBA_TK), lambda bi, h, j, s: (0, 0)),
            ],
            out_specs=pl.BlockSpec((1, MOBA_TQ, HEAD_DIM), lambda bi, h, j, s: (bi, j, h)),
        ),
        out_shape=jax.ShapeDtypeStruct((b, t, D_MOBA), F32),
        compiler_params=_cparams(("parallel", "parallel", "arbitrary")),
        name="moba_attn",
    )(slopes, proj3, proj3, proj3, kmean_pad, fkey, causal)


def _gelu_tanh(x):
    c = np.float32(np.sqrt(2.0 / np.pi))
    return x * (0.5 * (1.0 + jnp.tanh(c * (x + 0.044715 * (x * x * x)))))


def _cmp_kernel(x_ref, w1c_ref, w1_ref, pos_ref, w2_ref, o_ref):
    x = x_ref[0, 0, 0]
    nch = x.shape[0]
    ab = _dot(x, w1c_ref[0])
    first = ab[:, :HEAD_DIM]
    second = pltpu.roll(ab[:, HEAD_DIM:], shift=nch - 1, axis=0)
    p_hi, p_lo = _split2(pos_ref[0])
    w_hi, w_lo = _split2(w1_ref[0])
    posc = _dot(p_hi, w_hi) + _dot(p_hi, w_lo) + _dot(p_lo, w_hi)
    hid = _gelu_tanh(first + second + posc[0:1])
    o_ref[0, 0, 0] = _dot(hid.astype(BF16), w2_ref[0]).astype(BF16)


def _compress(xc, w1cat, w1, pos, w2):
    _, b, g, nch, _ = xc.shape
    hid = HEAD_DIM
    return pl.pallas_call(
        _cmp_kernel,
        grid=(2, b, g),
        in_specs=[
            pl.BlockSpec((1, 1, 1, nch, CMP_STRIDE * HEAD_DIM), lambda s, bi, gi: (s, bi, gi, 0, 0)),
            pl.BlockSpec((1, CMP_STRIDE * HEAD_DIM, 2 * hid), lambda s, bi, gi: (s, 0, 0)),
            pl.BlockSpec((1, CMP_LEN * HEAD_DIM, hid), lambda s, bi, gi: (s, 0, 0)),
            pl.BlockSpec((1, 8, CMP_LEN * HEAD_DIM), lambda s, bi, gi: (s, 0, 0)),
            pl.BlockSpec((1, hid, HEAD_DIM), lambda s, bi, gi: (s, 0, 0)),
        ],
        out_specs=pl.BlockSpec((1, 1, 1, nch, HEAD_DIM), lambda s, bi, gi: (s, bi, gi, 0, 0)),
        out_shape=jax.ShapeDtypeStruct((2, b, g, nch, HEAD_DIM), BF16),
        compiler_params=_cparams(("parallel", "parallel", "parallel")),
        name="nsa_compress",
    )(xc, w1cat, w1, pos, w2)


CMP_TQ = 256
NSA_HALF_BLOCKS = LANES // 2


def _nsa_cmp_kernel(n_slc, slopes_ref, q_ref, kc_ref, vc_ref, ov_ref, oc_ref, sb_ref):
    g = pl.program_id(1)
    i = pl.program_id(2)
    tq = CMP_TQ
    kc = kc_ref[0, 0, 0]
    vc = vc_ref[0, 0, 0]
    nch = kc.shape[0]
    t = i * tq + lax.broadcasted_iota(I32, (tq, nch), 0)
    cend = lax.broadcasted_iota(I32, (tq, nch), 1) * CMP_STRIDE + (CMP_LEN - 1)
    valid = cend <= t
    dist = (t - cend).astype(F32)
    psum = jnp.zeros((tq, nch), F32)
    for r in range(NSA_REP):
        slope = slopes_ref[g * NSA_REP + r]
        q = q_ref[0, :, r * HEAD_DIM:(r + 1) * HEAD_DIM]
        s = jnp.where(valid, _dot_nt(q, kc) * SCALE - slope * dist, NEG)
        m = jnp.max(s, axis=1, keepdims=True)
        m = jnp.where(m > 0.5 * NEG, m, 0.0)
        e = jnp.where(valid, jnp.exp(s - m), 0.0)
        p = e / jnp.maximum(jnp.sum(e, axis=1, keepdims=True), 1e-30)
        oc_ref[0, :, r * HEAD_DIM:(r + 1) * HEAD_DIM] = _dot(p.astype(BF16), vc)
        psum = psum + p
    p_hi = psum.astype(BF16)
    rem = psum - p_hi.astype(F32)
    p_mid = rem.astype(BF16)
    p_lo = (rem - p_mid.astype(F32)).astype(BF16)
    ov = ov_ref[...]
    imp = _dot(p_hi, ov) + _dot(p_mid, ov) + _dot(p_lo, ov)
    lane = lax.broadcasted_iota(I32, (tq, LANES), 1)
    bt = (i * tq + lax.broadcasted_iota(I32, (tq, LANES), 0)) // SLC_BLOCK
    forced = (lane == 0) | (lane == bt) | (lane == bt - 1)
    pri = jnp.where(lane > bt, -jnp.inf, jnp.where(forced, jnp.inf, imp))
    sel = _pick_topk(pri, lane, n_slc, LANES)
    bias = jnp.where(sel, 0.0, NEG)
    low = lane < NSA_HALF_BLOCKS
    sb_ref[0, 0, 0] = jnp.where(low, bias, 0.0).astype(BF16)
    sb_ref[0, 0, 1] = jnp.where(low, pltpu.roll(bias, NSA_HALF_BLOCKS, axis=1), 0.0).astype(BF16)


def _nsa_cmp(proj3, kvc, overlap, slopes):
    b, t, _ = proj3.shape
    nch = kvc.shape[3]
    n_slc = min(SLC_TOPK, t // SLC_BLOCK)
    qblk = COL_NQ // (NSA_REP * HEAD_DIM)
    gw = NSA_REP * HEAD_DIM
    return pl.pallas_call(
        functools.partial(_nsa_cmp_kernel, n_slc),
        grid_spec=pltpu.PrefetchScalarGridSpec(
            num_scalar_prefetch=1,
            grid=(b, NSA_KV_HEADS, t // CMP_TQ),
            in_specs=[
                pl.BlockSpec((1, CMP_TQ, gw), lambda bi, g, i, s: (bi, i, qblk + g)),
                pl.BlockSpec((1, 1, 1, nch, HEAD_DIM), lambda bi, g, i, s: (0, bi, g, 0, 0)),
                pl.BlockSpec((1, 1, 1, nch, HEAD_DIM), lambda bi, g, i, s: (1, bi, g, 0, 0)),
                pl.BlockSpec((nch, LANES), lambda bi, g, i, s: (0, 0)),
            ],
            out_specs=[
                pl.BlockSpec((1, CMP_TQ, gw), lambda bi, g, i, s: (bi, i, g)),
                pl.BlockSpec((1, 1, 2, CMP_TQ, LANES), lambda bi, g, i, s: (bi, g, 0, i, 0)),
            ],
        ),
        out_shape=[jax.ShapeDtypeStruct((b, t, D_NSA), F32),
                   jax.ShapeDtypeStruct((b, NSA_KV_HEADS, 2, t, LANES), BF16)],
        compiler_params=_cparams(("parallel", "parallel", "arbitrary")),
        name="nsa_cmp_attn",
    )(slopes, proj3, kvc, kvc, overlap)


NSA_TQ = 256
SEL_TK = 512
WIN_KEYS = WINDOW + NSA_TQ
HALF_TILES = NSA_HALF_BLOCKS * SLC_BLOCK // SEL_TK


def _nsa_main_kernel(slopes_ref, q_ref, ks_ref, vs_ref, kw_ref, vw_ref, fk_ref, sb_ref, oc_ref, gt_ref, o_ref):
    g = pl.program_id(1)
    i = pl.program_id(2)
    tq, rows = NSA_TQ, NSA_REP * NSA_TQ
    t0 = i * tq
    tile4 = lambda a: jnp.concatenate([a] * NSA_REP, axis=0)
    qs = jnp.concatenate([q_ref[0, :, r * HEAD_DIM:(r + 1) * HEAD_DIM] for r in range(NSA_REP)], axis=0)
    lane = lax.broadcasted_iota(I32, (rows, LANES), 1)
    slope_raw = jnp.concatenate(
        [jnp.full((tq, LANES), slopes_ref[g * NSA_REP + r] / SCALE, F32) for r in range(NSA_REP)], axis=0)
    alibi = _alibi_cols(slope_raw, lane, NSA_HALF_BLOCKS).astype(BF16)
    qx_lo = jnp.concatenate([qs, tile4(sb_ref[0, 0, 0]) + alibi], axis=1)
    qx_hi = jnp.concatenate([qs, tile4(sb_ref[0, 0, 1]) + alibi], axis=1)
    qx_win = jnp.concatenate([qs, alibi], axis=1)
    ones = jnp.ones((WIN_KEYS, HEAD_DIM), BF16)

    def logits(n):
        start = pl.multiple_of(n * SEL_TK, SEL_TK)
        kx = jnp.concatenate([ks_ref[0, pl.ds(start, SEL_TK), :], fk_ref[pl.ds(start, SEL_TK), :]], axis=1)
        return _dot_nt(jnp.where(n < HALF_TILES, qx_lo, qx_hi), kx)

    def values(n):
        start = pl.multiple_of(n * SEL_TK, SEL_TK)
        return jnp.concatenate([vs_ref[0, pl.ds(start, SEL_TK), :], ones[:SEL_TK]], axis=1)

    last = t0 // SEL_TK
    rq = lax.broadcasted_iota(I32, (tq, SEL_TK), 0)
    ck = lax.broadcasted_iota(I32, (tq, SEL_TK), 1)
    causal = jnp.where(ck - rq <= t0 - last * SEL_TK, 0.0, NEG)
    acc_s = _causal_flash(logits, values, last, tile4(causal), rows)

    wstart = pl.multiple_of(jnp.maximum(t0 - WINDOW, 0), NSA_TQ)
    kxw = jnp.concatenate([kw_ref[0, pl.ds(wstart, WIN_KEYS), :], fk_ref[pl.ds(wstart, WIN_KEYS), :]], axis=1)
    vxw = jnp.concatenate([vw_ref[0, pl.ds(wstart, WIN_KEYS), :], ones], axis=1)
    rel = (lax.broadcasted_iota(I32, (tq, WIN_KEYS), 1) + (wstart - t0)
           - lax.broadcasted_iota(I32, (tq, WIN_KEYS), 0))
    wbias = jnp.where(rel <= 0, jnp.where(rel > -WINDOW, 0.0, NEG), NEG)
    s = _dot_nt(qx_win, kxw) + tile4(wbias)
    p = jnp.exp2((s - jnp.max(s, axis=1, keepdims=True)) * EXP2_SCALE)
    acc_w = _dot(p.astype(BF16), vxw)

    o_s = acc_s[:, :HEAD_DIM] / acc_s[:, HEAD_DIM:HEAD_DIM + 1]
    o_w = acc_w[:, :HEAD_DIM] / acc_w[:, HEAD_DIM:HEAD_DIM + 1]
    gt = _sigmoid(gt_ref[0])
    for r in range(NSA_REP):
        dsl = slice(r * HEAD_DIM, (r + 1) * HEAD_DIM)
        rsl = slice(r * tq, (r + 1) * tq)
        o_ref[0, :, dsl] = (gt[:, 3 * r:3 * r + 1] * oc_ref[0, :, dsl]
                            + gt[:, 3 * r + 1:3 * r + 2] * o_s[rsl]
                            + gt[:, 3 * r + 2:3 * r + 3] * o_w[rsl])


def _nsa_main(proj3, selbias, o_c, gates3, slopes):
    b, t, _ = proj3.shape
    gw = NSA_REP * HEAD_DIM
    qblk = COL_NQ // gw
    cb = lambda col: col // HEAD_DIM
    kv_spec = lambda col: pl.BlockSpec((1, t, HEAD_DIM), lambda bi, g, i, s: (bi, 0, cb(col) + g))
    assert t % SEL_TK == 0 and t >= WIN_KEYS and t // SLC_BLOCK <= 2 * NSA_HALF_BLOCKS
    fkey = _key_features(t, SLC_BLOCK, NSA_HALF_BLOCKS, NSA_HALF_BLOCKS)
    return pl.pallas_call(
        _nsa_main_kernel,
        grid_spec=pltpu.PrefetchScalarGridSpec(
            num_scalar_prefetch=1,
            grid=(b, NSA_KV_HEADS, t // NSA_TQ),
            in_specs=[
                pl.BlockSpec((1, NSA_TQ, gw), lambda bi, g, i, s: (bi, i, qblk + g)),
                kv_spec(COL_KS), kv_spec(COL_VS), kv_spec(COL_KW), kv_spec(COL_VW),
                pl.BlockSpec((t, LANES), lambda bi, g, i, s: (0, 0)),
                pl.BlockSpec((1, 1, 2, NSA_TQ, LANES), lambda bi, g, i, s: (bi, g, 0, i, 0)),
                pl.BlockSpec((1, NSA_TQ, gw), lambda bi, g, i, s: (bi, i, g)),
                pl.BlockSpec((1, NSA_TQ, LANES), lambda bi, g, i, s: (bi, i, g)),
            ],
            out_specs=pl.BlockSpec((1, NSA_TQ, gw), lambda bi, g, i, s: (bi, i, g)),
        ),
        out_shape=jax.ShapeDtypeStruct((b, t, D_NSA), F32),
        compiler_params=_cparams(("parallel", "parallel", "arbitrary")),
        name="nsa_main",
    )(slopes, proj3, proj3, proj3, proj3, proj3, fkey, selbias, o_c, gates3)


def _outproj_kernel(om_ref, on_ref, x_ref, gm_ref, gn_ref, wt_ref, wb_ref, o_ref):
    a = _rms(om_ref[...], gm_ref[...]).astype(BF16)
    c = _rms(on_ref[...], gn_ref[...]).astype(BF16)
    o_ref[...] = x_ref[...] + (_dot(a, wt_ref[...]) + _dot(c, wb_ref[...]))


def _outproj(o_m, o_n, x2, gm, gn, w_top, w_bot):
    n = x2.shape[0]
    tm = min(512, n)
    row = lambda w: pl.BlockSpec((tm, w), lambda i: (i, 0))
    const = lambda r, c: pl.BlockSpec((r, c), lambda i: (0, 0))
    return pl.pallas_call(
        _outproj_kernel,
        grid=(n // tm,),
        in_specs=[row(D_MOBA), row(D_NSA), row(D_MODEL), const(1, D_MOBA), const(1, D_NSA),
                  const(D_MOBA, D_MODEL), const(D_NSA, D_MODEL)],
        out_specs=row(D_MODEL),
        out_shape=jax.ShapeDtypeStruct((n, D_MODEL), F32),
        compiler_params=_cparams(("parallel",)),
        name="outproj",
    )(o_m, o_n, x2, gm, gn, w_top, w_bot)


ROUTER_TM = 512


def _router_kernel(x_ref, g_ref, wr_ref, bias_ref, tri_ref, h_ref, eidx_ref, wgt_ref, pos_ref, cnt_ref, carry):
    step = pl.program_id(0)
    tm = ROUTER_TM

    @pl.when(step == 0)
    def _():
        carry[...] = jnp.zeros_like(carry)

    h = _rms(x_ref[...], g_ref[...])
    h_ref[...] = h
    h_hi, h_lo = _split2(h)
    w_hi, w_lo = _split2(wr_ref[...])
    logits = _dot_nt(w_hi, h_hi) + _dot_nt(w_hi, h_lo) + _dot_nt(w_lo, h_hi)
    aff = _sigmoid(logits)
    biased = aff + bias_ref[...][:, 0:1]

    b3 = biased.reshape(N_GROUPS, GROUP_SIZE, tm)
    sub = lax.broadcasted_iota(I32, b3.shape, 1)
    m1 = jnp.max(b3, axis=1, keepdims=True)
    i1 = jnp.min(jnp.where(b3 == m1, sub, GROUP_SIZE), axis=1, keepdims=True)
    m2 = jnp.max(jnp.where(sub == i1, -jnp.inf, b3), axis=1, keepdims=True)
    gscore = (m1 + m2).reshape(N_GROUPS, tm)

    giota = lax.broadcasted_iota(I32, (N_GROUPS, tm), 0)
    gsel = jnp.zeros((N_GROUPS, tm), jnp.bool_)
    gs = gscore
    for _ in range(TOPK_GROUPS):
        m = jnp.max(gs, axis=0, keepdims=True)
        idx = jnp.min(jnp.where(gs == m, giota, N_GROUPS), axis=0, keepdims=True)
        pick = giota == idx
        gsel = gsel | pick
        gs = jnp.where(pick, -jnp.inf, gs)
    emask = jnp.broadcast_to(jnp.where(gsel, 1.0, 0.0)[:, None, :], (N_GROUPS, GROUP_SIZE, tm)).reshape(N_EXPERTS, tm)

    eiota = lax.broadcasted_iota(I32, (N_EXPERTS, tm), 0)
    cand = jnp.where(emask > 0.5, biased, -jnp.inf)
    multi = jnp.zeros((N_EXPERTS, tm), F32)
    idxs, wts = [], []
    for _ in range(TOP_K):
        m = jnp.max(cand, axis=0, keepdims=True)
        idx = jnp.min(jnp.where(cand == m, eiota, N_EXPERTS), axis=0, keepdims=True)
        pick = eiota == idx
        idxs.append(idx)
        wts.append(jnp.sum(jnp.where(pick, aff, 0.0), axis=0, keepdims=True))
        multi = jnp.where(pick, 1.0, multi)
        cand = jnp.where(pick, -jnp.inf, cand)
    wsum = wts[0]
    for w in wts[1:]:
        wsum = wsum + w

    base = carry[...][:, 0:1]
    cum = _dot(multi.astype(BF16), tri_ref[...]) + base
    poss = [jnp.sum(jnp.where(eiota == idx, cum, 0.0), axis=0, keepdims=True) for idx in idxs]
    total = base + jnp.sum(multi, axis=1, keepdims=True)
    carry[...] = jnp.broadcast_to(total, carry.shape)
    cnt_ref[...] = jnp.broadcast_to(total, cnt_ref.shape)
    eidx_ref[...] = jnp.concatenate(idxs, axis=0)
    wgt_ref[...] = jnp.concatenate([w / wsum * ROUTED_SCALE for w in wts], axis=0)
    pos_ref[...] = jnp.concatenate(poss, axis=0).astype(I32)


def _router(x1, g, wr_t, bias, tri):
    n = x1.shape[0]
    tm = ROUTER_TM
    const = lambda r, c: pl.BlockSpec((r, c), lambda i: (0, 0))
    tok = lambda: pl.BlockSpec((TOP_K, tm), lambda i: (0, i))
    return pl.pallas_call(
        _router_kernel,
        grid=(n // tm,),
        in_specs=[pl.BlockSpec((tm, D_MODEL), lambda i: (i, 0)), const(1, D_MODEL), const(N_EXPERTS, D_MODEL),
                  const(N_EXPERTS, LANES), const(tm, tm)],
        out_specs=[pl.BlockSpec((tm, D_MODEL), lambda i: (i, 0)), tok(), tok(), tok(), const(N_EXPERTS, LANES)],
        out_shape=[jax.ShapeDtypeStruct((n, D_MODEL), F32), jax.ShapeDtypeStruct((TOP_K, n), I32),
                   jax.ShapeDtypeStruct((TOP_K, n), F32), jax.ShapeDtypeStruct((TOP_K, n), I32),
                   jax.ShapeDtypeStruct((N_EXPERTS, LANES), F32)],
        scratch_shapes=[pltpu.VMEM((N_EXPERTS, LANES), F32)],
        compiler_params=_cparams(("arbitrary",)),
        name="router",
    )(x1, g, wr_t, bias, tri)


DISPATCH_TM = 256
ZERO_ROWS = EXPERT_TILE + 8


def _dispatch_kernel(pad_ref, tail_ref, dest_ref, h_ref, xs_hbm, zeros_vmem, sem, zsem):
    step = pl.program_id(0)
    tm = DISPATCH_TM

    @pl.when(step == 0)
    def _():
        zeros_vmem[...] = jnp.zeros_like(zeros_vmem)

        def fill(e, _):
            start = pl.multiple_of((pad_ref[e] // 8) * 8, 8)
            pltpu.make_async_copy(zeros_vmem, xs_hbm.at[pl.ds(start, ZERO_ROWS), :], zsem).start()
            return 0

        lax.fori_loop(0, N_EXPERTS, fill, 0)

        def drain(e, _):
            pltpu.make_async_copy(zeros_vmem, xs_hbm.at[pl.ds(0, ZERO_ROWS), :], zsem).wait()
            return 0

        lax.fori_loop(0, N_EXPERTS, drain, 0)

        tail_start = tail_ref[0]
        n_tail = (xs_hbm.shape[0] - tail_start) // EXPERT_TILE

        def fill_tail(i, _):
            start = pl.multiple_of(tail_start + i * EXPERT_TILE, EXPERT_TILE)
            pltpu.make_async_copy(zeros_vmem.at[pl.ds(0, EXPERT_TILE), :],
                                  xs_hbm.at[pl.ds(start, EXPERT_TILE), :], zsem).start()
            return 0

        lax.fori_loop(0, n_tail, fill_tail, 0)

        def drain_tail(i, _):
            pltpu.make_async_copy(zeros_vmem.at[pl.ds(0, EXPERT_TILE), :],
                                  xs_hbm.at[pl.ds(0, EXPERT_TILE), :], zsem).wait()
            return 0

        lax.fori_loop(0, n_tail, drain_tail, 0)

    def issue(j, _):
        for k in range(TOP_K):
            pltpu.make_async_copy(h_ref.at[pl.ds(j, 1), :],
                                  xs_hbm.at[pl.ds(dest_ref[j * TOP_K + k], 1), :], sem).start()
        return 0

    lax.fori_loop(0, tm, issue, 0)

    def drain(k, _):
        pltpu.make_async_copy(h_ref, xs_hbm.at[pl.ds(0, tm), :], sem).wait()
        return 0

    lax.fori_loop(0, TOP_K, drain, 0)


def _dispatch(h2, dest_flat, pad_start, tail_start, p_alloc):
    n = h2.shape[0]
    tm = min(DISPATCH_TM, n)
    return pl.pallas_call(
        _dispatch_kernel,
        grid_spec=pltpu.PrefetchScalarGridSpec(
            num_scalar_prefetch=2,
            grid=(n // tm,),
            in_specs=[pl.BlockSpec((tm * TOP_K,), lambda i, s, u: (i,), memory_space=pltpu.SMEM),
                      pl.BlockSpec((tm, D_MODEL), lambda i, s, u: (i, 0))],
            out_specs=pl.BlockSpec(memory_space=pl.ANY),
            scratch_shapes=[pltpu.VMEM((ZERO_ROWS, D_MODEL), F32), pltpu.SemaphoreType.DMA,
                            pltpu.SemaphoreType.DMA],
        ),
        out_shape=jax.ShapeDtypeStruct((p_alloc, D_MODEL), F32),
        compiler_params=_cparams(("arbitrary",)),
        name="moe_dispatch",
    )(pad_start, tail_start, dest_flat, h2)


def _moe_kernel(ce_ref, nu_ref, x_ref, wg_ref, wu_ref, wd_ref, y_ref, wg_b, wu_b, wd_b):
    c = pl.program_id(0)

    @pl.when(c < nu_ref[0])
    def _():
        @pl.when((c == 0) | (ce_ref[c] != ce_ref[jnp.maximum(c - 1, 0)]))
        def _():
            wg_b[...] = wg_ref[0].astype(BF16)
            wu_b[...] = wu_ref[0].astype(BF16)
            wd_b[...] = wd_ref[0].astype(BF16)

        x = x_ref[...].astype(BF16)
        gp = _dot(x, wg_b[...])
        up = _dot(x, wu_b[...])
        y_ref[...] = _dot((gp * _sigmoid(gp) * up).astype(BF16), wd_b[...])

    @pl.when(c >= nu_ref[0])
    def _():
        y_ref[...] = jnp.zeros_like(y_ref)


def _moe(xs, chunk_e, n_used, w_gate_e, w_up_e, w_down_e, n_chunks):
    c = EXPERT_TILE
    row = lambda i, ce, nu: (jnp.minimum(i, nu[0] - 1), 0)
    wsel = lambda i, ce, nu: (ce[jnp.minimum(i, nu[0] - 1)], 0, 0)
    return pl.pallas_call(
        _moe_kernel,
        grid_spec=pltpu.PrefetchScalarGridSpec(
            num_scalar_prefetch=2,
            grid=(n_chunks,),
            in_specs=[pl.BlockSpec((c, D_MODEL), row),
                      pl.BlockSpec((1, D_MODEL, D_EXPERT), wsel),
                      pl.BlockSpec((1, D_MODEL, D_EXPERT), wsel),
                      pl.BlockSpec((1, D_EXPERT, D_MODEL), wsel)],
            out_specs=pl.BlockSpec((c, D_MODEL), lambda i, ce, nu: (i, 0)),
            scratch_shapes=[pltpu.VMEM((D_MODEL, D_EXPERT), BF16), pltpu.VMEM((D_MODEL, D_EXPERT), BF16),
                            pltpu.VMEM((D_EXPERT, D_MODEL), BF16)],
        ),
        out_shape=jax.ShapeDtypeStruct((n_chunks * c, D_MODEL), F32),
        compiler_params=_cparams(("arbitrary",)),
        name="moe_experts",
    )(chunk_e, n_used, xs, w_gate_e, w_up_e, w_down_e)


COMBINE_TM = 64


def _combine_kernel(dest_ref, w_ref, y_hbm, o_ref, rows, sem):
    tm = COMBINE_TM

    def issue(j, _):
        for k in range(TOP_K):
            pltpu.make_async_copy(y_hbm.at[pl.ds(dest_ref[j * TOP_K + k], 1), :],
                                  rows.at[pl.ds(k * tm + j, 1), :], sem).start()
        return 0

    lax.fori_loop(0, tm, issue, 0)

    pltpu.make_async_copy(y_hbm.at[pl.ds(0, tm * TOP_K), :], rows, sem).wait()
    w = w_ref[...]
    acc = w[:, 0:1] * rows[0:tm, :]
    for k in range(1, TOP_K):
        acc = acc + w[:, k:k + 1] * rows[k * tm:(k + 1) * tm, :]
    o_ref[...] = acc


def _combine(ys, dest_flat, w_tok):
    n = w_tok.shape[0]
    tm = COMBINE_TM
    return pl.pallas_call(
        _combine_kernel,
        grid=(n // tm,),
        in_specs=[pl.BlockSpec((tm * TOP_K,), lambda i: (i,), memory_space=pltpu.SMEM),
                  pl.BlockSpec((tm, TOP_K), lambda i: (i, 0)),
                  pl.BlockSpec(memory_space=pl.ANY)],
        out_specs=pl.BlockSpec((tm, D_MODEL), lambda i: (i, 0)),
        out_shape=jax.ShapeDtypeStruct((n, D_MODEL), F32),
        scratch_shapes=[pltpu.VMEM((tm * TOP_K, D_MODEL), F32), pltpu.SemaphoreType.DMA],
        compiler_params=_cparams(("arbitrary",)),
        name="moe_combine",
    )(dest_flat, w_tok, ys)


def _final_kernel(last, x1_ref, rt_ref, h_ref, p_ref, wg_ref, wu_ref, wd_ref, gp_ref, wple_ref, wpg_ref, gf_ref, o_ref):
    hb = h_ref[...].astype(BF16)
    gp = _dot(hb, wg_ref[...])
    up = _dot(hb, wu_ref[...])
    shared = _dot((gp * _sigmoid(gp) * up).astype(BF16), wd_ref[...])
    x2 = x1_ref[...] + (rt_ref[...] + shared)
    gate = _sigmoid(_dot(_rms(x2, gp_ref[...]).astype(BF16), wpg_ref[...]))
    x3 = x2 + _dot(p_ref[...].astype(BF16), wple_ref[...]) * gate
    o_ref[...] = _rms(x3, gf_ref[...]) if last else x3


def _final(x1, routed, h2, p2, wgs, wus, wds, g_ple, w_ple, w_pg, g_final, last):
    n = x1.shape[0]
    tm = min(128, n)
    row = lambda w: pl.BlockSpec((tm, w), lambda i: (i, 0))
    const = lambda r, c: pl.BlockSpec((r, c), lambda i: (0, 0))
    return pl.pallas_call(
        functools.partial(_final_kernel, last),
        grid=(n // tm,),
        in_specs=[row(D_MODEL), row(D_MODEL), row(D_MODEL), row(PLE_DIM),
                  const(D_MODEL, D_EXPERT), const(D_MODEL, D_EXPERT), const(D_EXPERT, D_MODEL),
                  const(1, D_MODEL), const(PLE_DIM, D_MODEL), const(D_MODEL, D_MODEL), const(1, D_MODEL)],
        out_specs=row(D_MODEL),
        out_shape=jax.ShapeDtypeStruct((n, D_MODEL), F32),
        compiler_params=_cparams(("parallel",)),
        name="shared_ple_final",
    )(x1, routed, h2, p2, wgs, wus, wds, g_ple, w_ple, w_pg, g_final)


def _alibi_slopes(n):
    return jnp.asarray(2.0 ** (-8.0 * np.arange(1, n + 1) / n), dtype=F32)


def _layer(x, p_i, norm_mix_g, w_in, moba_out_g, nsa_out_g, cmp_pos_k, cmp_w1_k, cmp_w2_k,
           cmp_pos_v, cmp_w1_v, cmp_w2_v, w_out, norm_ffn_g, w_router, router_bias,
           w_gate_e, w_up_e, w_down_e, w_gate_s, w_up_s, w_down_s, norm_ple_g, w_ple, w_ple_gate,
           norm_final_g, last):
    b, t, _ = x.shape
    n = b * t
    slopes = _alibi_slopes(N_HEADS_MOBA + N_HEADS_NSA)
    x2 = x.reshape(n, D_MODEL)

    w_main = w_in[:, :D_PROJ].astype(BF16)
    wg = w_in[:, COL_GATE:].reshape(D_MODEL, NSA_KV_HEADS, NSA_REP * 3)
    w_gate = jnp.pad(wg, ((0, 0), (0, 0), (0, LANES - NSA_REP * 3))).reshape(D_MODEL, 2 * LANES).astype(BF16)
    proj, gates = _inproj(x2, norm_mix_g.reshape(1, D_MODEL), w_main, w_gate)
    proj3 = proj.reshape(b, t, D_PROJ)
    gates3 = gates.reshape(b, t, 2 * LANES)

    nb = t // MOBA_BLOCK
    kmean = _kmean(proj3)
    kmean_pad = jnp.pad(kmean, ((0, 0), (0, LANES - nb), (0, 0)))
    o_m = _moba(proj3, kmean_pad, slopes[0::2])

    nch = t // CMP_STRIDE
    kvc = jnp.stack([proj3[:, :, COL_KC:COL_KC + D_KV], proj3[:, :, COL_VC:COL_VC + D_KV]])
    kvc = kvc.reshape(2, b, t, NSA_KV_HEADS, HEAD_DIM).transpose(0, 1, 3, 2, 4)
    kvc = kvc.reshape(2, b, NSA_KV_HEADS, nch, CMP_STRIDE * HEAD_DIM)
    w1 = jnp.stack([cmp_w1_k, cmp_w1_v])
    half = CMP_STRIDE * HEAD_DIM
    w1cat = jnp.concatenate([w1[:, :half], w1[:, half:]], axis=2).astype(BF16)
    pos = jnp.stack([cmp_pos_k, cmp_pos_v]).reshape(2, 1, CMP_LEN * HEAD_DIM)
    pos = jnp.broadcast_to(pos, (2, 8, CMP_LEN * HEAD_DIM))
    w2 = jnp.stack([cmp_w2_k, cmp_w2_v]).astype(BF16)
    kv_cmp = _compress(kvc, w1cat, w1, pos, w2)
    cs = np.arange(nch)[:, None] * CMP_STRIDE
    ss = np.arange(LANES)[None, :] * SLC_BLOCK
    overlap = jnp.asarray(((cs + CMP_LEN - 1 >= ss) & (cs <= ss + SLC_BLOCK - 1)).astype(np.float32), dtype=BF16)
    o_c, selbias = _nsa_cmp(proj3, kv_cmp, overlap, slopes[1::2])
    o_n = _nsa_main(proj3, selbias, o_c, gates3, slopes[1::2])

    w_out_b = w_out.astype(BF16)
    x1 = _outproj(o_m.reshape(n, D_MOBA), o_n.reshape(n, D_NSA), x2, moba_out_g.reshape(1, D_MOBA),
                  nsa_out_g.reshape(1, D_NSA), w_out_b[:D_MOBA], w_out_b[D_MOBA:])

    tri = jnp.asarray(np.triu(np.ones((ROUTER_TM, ROUTER_TM), np.float32), k=1), dtype=BF16)
    bias = jnp.broadcast_to(router_bias.astype(F32)[:, None], (N_EXPERTS, LANES))
    h2, eidx_t, wgt_t, pos_t, cnt = _router(x1, norm_ffn_g.reshape(1, D_MODEL), w_router.T, bias, tri)
    counts = cnt[:, 0].astype(I32)
    c = EXPERT_TILE
    pcounts = (counts + c - 1) // c * c
    pend = jnp.cumsum(pcounts)
    pstart = pend - pcounts
    experts = jnp.arange(N_EXPERTS, dtype=I32)
    seg_start = jnp.sum(jnp.where(eidx_t[..., None] == experts, pstart, 0), axis=-1)
    dest_flat = (seg_start + pos_t).T.reshape(n * TOP_K)
    n_chunks = -(-(n * TOP_K + N_EXPERTS * (c - 1)) // c)
    n_used = (pend[-1] // c).astype(I32).reshape(1)
    chunk_row = jnp.arange(n_chunks, dtype=I32)[:, None] * c
    chunk_e = jnp.minimum(jnp.sum((pend[None, :] <= chunk_row).astype(I32), axis=1), N_EXPERTS - 1)

    xs = _dispatch(h2, dest_flat, (pstart + counts).astype(I32), pend[-1:].astype(I32), (n_chunks + 2) * c)
    ys = _moe(xs, chunk_e, n_used, w_gate_e, w_up_e, w_down_e, n_chunks)
    routed = _combine(ys, dest_flat, wgt_t.T)

    return _final(x1, routed, h2, p_i.reshape(n, PLE_DIM), w_gate_s.astype(BF16), w_up_s.astype(BF16),
                  w_down_s.astype(BF16), norm_ple_g.reshape(1, D_MODEL), w_ple.astype(BF16),
                  w_ple_gate.astype(BF16), norm_final_g.reshape(1, D_MODEL), last).reshape(b, t, D_MODEL)


def kernel(x, p, norm_mix_g, w_in, moba_out_g, nsa_out_g, cmp_pos_k, cmp_w1_k, cmp_w2_k, cmp_pos_v, cmp_w1_v, cmp_w2_v, w_out, norm_ffn_g, w_router, router_bias, w_gate_e, w_up_e, w_down_e, w_gate_s, w_up_s, w_down_s, norm_ple_g, w_ple, w_ple_gate, norm_final_g):
    per_layer = (norm_mix_g, w_in, moba_out_g, nsa_out_g, cmp_pos_k, cmp_w1_k, cmp_w2_k, cmp_pos_v, cmp_w1_v,
                 cmp_w2_v, w_out, norm_ffn_g, w_router, router_bias, w_gate_e, w_up_e, w_down_e,
                 w_gate_s, w_up_s, w_down_s, norm_ple_g, w_ple, w_ple_gate)
    depth = p.shape[0]
    for i in range(depth):
        x = _layer(x, p[i], *(a[i] for a in per_layer), norm_final_g, i == depth - 1)
    return x
```

```python
import functools

import numpy as np
import jax
import jax.numpy as jnp
from jax import lax
from jax.experimental import pallas as pl
from jax.experimental.pallas import tpu as pltpu

F32 = jnp.float32
BF16 = jnp.bfloat16
I32 = jnp.int32

D_MODEL = 2048
HEAD_DIM = 128
N_HEADS_MOBA = 8
N_HEADS_NSA = 8
NSA_KV_HEADS = 2
NSA_REP = N_HEADS_NSA // NSA_KV_HEADS
MOBA_BLOCK = 256
MOBA_TOPK = 3
CMP_LEN = 32
CMP_STRIDE = 16
SLC_BLOCK = 64
SLC_TOPK = 16
WINDOW = 512
N_EXPERTS = 64
N_GROUPS = 8
GROUP_SIZE = N_EXPERTS // N_GROUPS
TOPK_GROUPS = 4
TOP_K = 8
D_EXPERT = 512
ROUTED_SCALE = 2.5
PLE_DIM = 256
RMS_EPS = 1e-6
D_MOBA = N_HEADS_MOBA * HEAD_DIM
D_NSA = N_HEADS_NSA * HEAD_DIM
D_KV = NSA_KV_HEADS * HEAD_DIM
COL_MQ, COL_MK, COL_MV, COL_NQ = 0, D_MOBA, 2 * D_MOBA, 3 * D_MOBA
COL_KC = COL_NQ + D_NSA
COL_VC, COL_KS, COL_VS, COL_KW, COL_VW = (COL_KC + D_KV * i for i in range(1, 6))
D_PROJ = COL_VW + D_KV
COL_GATE = D_PROJ

LANES = 128
V7X_VMEM_LIMIT = 56 * 1024 * 1024
NEG = -1e30

EXPERT_TILE = 512
SCALE = HEAD_DIM ** -0.5
NT = (((1,), (1,)), ((), ()))


def _cparams(sem):
    return pltpu.CompilerParams(dimension_semantics=sem, vmem_limit_bytes=V7X_VMEM_LIMIT)


def _dot(a, b):
    return jnp.dot(a, b, preferred_element_type=F32)


def _dot_nt(a, b):
    return lax.dot_general(a, b, NT, preferred_element_type=F32)


def _split2(a):
    hi = a.astype(BF16)
    lo = (a - hi.astype(F32)).astype(BF16)
    return hi, lo


def _rms(x, g):
    ms = jnp.mean(x * x, axis=-1, keepdims=True)
    return x * lax.rsqrt(ms + RMS_EPS) * g


def _sigmoid(x):
    return 1.0 / (1.0 + jnp.exp(-x))


EXP2_SCALE = SCALE * float(np.log2(np.e))


def _causal_flash(logits, values, n_full, last_bias, s_a, s_b, m_scr, acc_scr):
    def update(s_ref, n, bias=None):
        s = s_ref[...] if bias is None else s_ref[...] + bias
        m = m_scr[...]
        m_new = jnp.maximum(m, jnp.max(s, axis=1, keepdims=True))
        alpha = jnp.exp2((m - m_new) * EXP2_SCALE)
        p = jnp.exp2((s - m_new) * EXP2_SCALE)
        m_scr[...] = m_new
        acc_scr[...] = alpha * acc_scr[...] + _dot(p.astype(BF16), values(n))

    m_scr[...] = jnp.full(m_scr.shape, NEG, F32)
    acc_scr[...] = jnp.zeros(acc_scr.shape, F32)
    s_a[...] = logits(0)

    def pair(k, _):
        s_b[...] = logits(2 * k + 1)
        update(s_a, 2 * k)
        s_a[...] = logits(2 * k + 2)
        update(s_b, 2 * k + 1)
        return 0

    lax.fori_loop(0, n_full // 2, pair, 0)

    @pl.when(n_full % 2 == 1)
    def _():
        s_b[...] = logits(n_full)
        update(s_a, n_full - 1)
        update(s_b, n_full, last_bias)

    @pl.when(n_full % 2 == 0)
    def _():
        update(s_a, n_full, last_bias)


def _alibi_cols(slope_raw, lane, first):
    hi = slope_raw.astype(BF16).astype(F32)
    lo = (slope_raw - hi).astype(BF16).astype(F32)
    is_hi = (lane == first) | (lane == first + 2)
    is_lo = (lane == first + 1) | (lane == first + 3)
    return jnp.where(is_hi, hi, jnp.where(is_lo, lo, 0.0))


def _key_features(t, block, n_lanes_onehot, first):
    pos = np.arange(t)
    f = np.zeros((t, LANES), np.float32)
    f[pos, (pos // block) % n_lanes_onehot] = 1.0
    f[:, first] = f[:, first + 1] = 256 * (pos // 256)
    f[:, first + 2] = f[:, first + 3] = pos % 256
    return jnp.asarray(f, dtype=BF16)


def _pick_topk(score, lane, k, sentinel):
    sel = jnp.zeros(score.shape, jnp.bool_)
    g = score
    sentinel = float(sentinel)
    for _ in range(k):
        m = jnp.max(g, axis=1, keepdims=True)
        idx = jnp.min(jnp.where(g == m, lane, sentinel), axis=1, keepdims=True)
        idx = jnp.where(m > -jnp.inf, idx, sentinel)
        pick = lane == idx
        sel = sel | pick
        g = jnp.where(pick, -jnp.inf, g)
    return sel


def _inproj_kernel(x_ref, g_ref, w_ref, wg_ref, proj_ref, gate_ref, h_scr):
    @pl.when(pl.program_id(1) == 0)
    def _():
        hb = _rms(x_ref[...], g_ref[...]).astype(BF16)
        h_scr[...] = hb
        gate_ref[...] = _dot(hb, wg_ref[...])

    proj_ref[...] = _dot(h_scr[...], w_ref[...]).astype(BF16)


def _inproj(x2, g, w_main, w_gate):
    n = x2.shape[0]
    tm = min(1024, n)
    tn = 512
    return pl.pallas_call(
        _inproj_kernel,
        grid=(n // tm, D_PROJ // tn),
        in_specs=[
            pl.BlockSpec((tm, D_MODEL), lambda i, j: (i, 0)),
            pl.BlockSpec((1, D_MODEL), lambda i, j: (0, 0)),
            pl.BlockSpec((D_MODEL, tn), lambda i, j: (0, j)),
            pl.BlockSpec((D_MODEL, 2 * LANES), lambda i, j: (0, 0)),
        ],
        out_specs=[
            pl.BlockSpec((tm, tn), lambda i, j: (i, j)),
            pl.BlockSpec((tm, 2 * LANES), lambda i, j: (i, 0)),
        ],
        out_shape=[jax.ShapeDtypeStruct((n, D_PROJ), BF16), jax.ShapeDtypeStruct((n, 2 * LANES), F32)],
        scratch_shapes=[pltpu.VMEM((tm, D_MODEL), BF16)],
        compiler_params=_cparams(("parallel", "arbitrary")),
        name="inproj",
    )(x2, g, w_main, w_gate)


def _kmean_kernel(k_ref, o_ref):
    k = k_ref[0].astype(F32)
    o_ref[0] = jnp.mean(k.reshape(8, MOBA_BLOCK, D_MOBA), axis=1)


def _kmean(proj3):
    b, t, _ = proj3.shape
    nb = t // MOBA_BLOCK
    return pl.pallas_call(
        _kmean_kernel,
        grid=(b, nb // 8),
        in_specs=[pl.BlockSpec((1, 8 * MOBA_BLOCK, D_MOBA), lambda bi, j: (bi, j, COL_MK // D_MOBA))],
        out_specs=pl.BlockSpec((1, 8, D_MOBA), lambda bi, j: (bi, j, 0)),
        out_shape=jax.ShapeDtypeStruct((b, nb, D_MOBA), F32),
        compiler_params=_cparams(("parallel", "parallel")),
        name="moba_kmean",
    )(proj3)


MOBA_TQ = 2 * MOBA_BLOCK
MOBA_TK = 2 * MOBA_BLOCK
MOBA_ALIBI_LANE = 120


def _flash_scratch(rows, tk):
    return [pltpu.VMEM((rows, tk), F32), pltpu.VMEM((rows, tk), F32), pltpu.VMEM((rows, 1), F32),
            pltpu.VMEM((rows, 2 * HEAD_DIM), F32)]


def _moba_kernel(slopes_ref, q_ref, k_ref, v_ref, km_ref, fk_ref, causal_ref, o_ref, s_a, s_b, m_scr, acc_scr):
    h = pl.program_id(1)
    j = pl.program_id(2)
    tq, tk = MOBA_TQ, MOBA_TK
    q = q_ref[0]
    km_hi, km_lo = _split2(km_ref[0])
    gate = _dot_nt(q, km_hi) + _dot_nt(q, km_lo)
    lane = lax.broadcasted_iota(I32, (tq, LANES), 1)
    own = j * (tq // MOBA_BLOCK) + lax.broadcasted_iota(I32, (tq, LANES), 0) // MOBA_BLOCK
    sel = _pick_topk(jnp.where(lane < own, gate, -jnp.inf), lane.astype(F32), MOBA_TOPK, LANES)
    sel = sel | (lane == own)
    slope_raw = jnp.full((tq, LANES), slopes_ref[h] / SCALE, F32)
    extra = jnp.where(lane < MOBA_ALIBI_LANE, jnp.where(sel, 0.0, NEG),
                      _alibi_cols(slope_raw, lane, MOBA_ALIBI_LANE))
    qx = jnp.concatenate([q, extra.astype(BF16)], axis=1)
    ones = jnp.ones((tk, HEAD_DIM), BF16)

    def logits(n):
        start = pl.multiple_of(n * tk, tk)
        kx = jnp.concatenate([k_ref[0, pl.ds(start, tk), :], fk_ref[pl.ds(start, tk), :]], axis=1)
        return _dot_nt(qx, kx)

    def values(n):
        start = pl.multiple_of(n * tk, tk)
        return jnp.concatenate([v_ref[0, pl.ds(start, tk), :], ones], axis=1)

    _causal_flash(logits, values, j, causal_ref[...], s_a, s_b, m_scr, acc_scr)
    o_ref[0] = acc_scr[:, :HEAD_DIM] / acc_scr[:, HEAD_DIM:HEAD_DIM + 1]


def _moba(proj3, kmean_pad, slopes):
    b, t, _ = proj3.shape
    assert t // MOBA_BLOCK <= MOBA_ALIBI_LANE and t % MOBA_TQ == 0
    qb, kb, vb = COL_MQ // HEAD_DIM, COL_MK // HEAD_DIM, COL_MV // HEAD_DIM
    fkey = _key_features(t, MOBA_BLOCK, MOBA_ALIBI_LANE, MOBA_ALIBI_LANE)
    causal = jnp.asarray(np.where(np.arange(MOBA_TK)[None, :] <= np.arange(MOBA_TQ)[:, None], 0.0, NEG), dtype=F32)
    return pl.pallas_call(
        _moba_kernel,
        grid_spec=pltpu.PrefetchScalarGridSpec(
            num_scalar_prefetch=1,
            grid=(b, N_HEADS_MOBA, t // MOBA_TQ),
            in_specs=[
                pl.BlockSpec((1, MOBA_TQ, HEAD_DIM), lambda bi, h, j, s: (bi, j, qb + h)),
                pl.BlockSpec((1, t, HEAD_DIM), lambda bi, h, j, s: (bi, 0, kb + h)),
                pl.BlockSpec((1, t, HEAD_DIM), lambda bi, h, j, s: (bi, 0, vb + h)),
                pl.BlockSpec((1, LANES, HEAD_DIM), lambda bi, h, j, s: (bi, 0, h)),
                pl.BlockSpec((t, LANES), lambda bi, h, j, s: (0, 0)),
                pl.BlockSpec((MOBA_TQ, MOBA_TK), lambda bi, h, j, s: (0, 0)),
            ],
            out_specs=pl.BlockSpec((1, MOBA_TQ, HEAD_DIM), lambda bi, h, j, s: (bi, j, h)),
            scratch_shapes=_flash_scratch(MOBA_TQ, MOBA_TK),
        ),
        out_shape=jax.ShapeDtypeStruct((b, t, D_MOBA), F32),
        compiler_params=_cparams(("parallel", "parallel", "arbitrary")),
        name="moba_attn",
    )(slopes, proj3, proj3, proj3, kmean_pad, fkey, causal)


def _gelu_tanh(x):
    c = np.float32(np.sqrt(2.0 / np.pi))
    return x * (0.5 * (1.0 + jnp.tanh(c * (x + 0.044715 * (x * x * x)))))


def _cmp_kernel(x_ref, w1c_ref, w1_ref, pos_ref, w2_ref, o_ref):
    x = x_ref[0, 0, 0]
    nch = x.shape[0]
    ab = _dot(x, w1c_ref[0])
    first = ab[:, :HEAD_DIM]
    second = pltpu.roll(ab[:, HEAD_DIM:], shift=nch - 1, axis=0)
    p_hi, p_lo = _split2(pos_ref[0])
    w_hi, w_lo = _split2(w1_ref[0])
    posc = _dot(p_hi, w_hi) + _dot(p_hi, w_lo) + _dot(p_lo, w_hi)
    hid = _gelu_tanh(first + second + posc[0:1])
    o_ref[0, 0, 0] = _dot(hid.astype(BF16), w2_ref[0]).astype(BF16)


def _compress(xc, w1cat, w1, pos, w2):
    _, b, g, nch, _ = xc.shape
    hid = HEAD_DIM
    return pl.pallas_call(
        _cmp_kernel,
        grid=(2, b, g),
        in_specs=[
            pl.BlockSpec((1, 1, 1, nch, CMP_STRIDE * HEAD_DIM), lambda s, bi, gi: (s, bi, gi, 0, 0)),
            pl.BlockSpec((1, CMP_STRIDE * HEAD_DIM, 2 * hid), lambda s, bi, gi: (s, 0, 0)),
            pl.BlockSpec((1, CMP_LEN * HEAD_DIM, hid), lambda s, bi, gi: (s, 0, 0)),
            pl.BlockSpec((1, 8, CMP_LEN * HEAD_DIM), lambda s, bi, gi: (s, 0, 0)),
            pl.BlockSpec((1, hid, HEAD_DIM), lambda s, bi, gi: (s, 0, 0)),
        ],
        out_specs=pl.BlockSpec((1, 1, 1, nch, HEAD_DIM), lambda s, bi, gi: (s, bi, gi, 0, 0)),
        out_shape=jax.ShapeDtypeStruct((2, b, g, nch, HEAD_DIM), BF16),
        compiler_params=_cparams(("parallel", "parallel", "parallel")),
        name="nsa_compress",
    )(xc, w1cat, w1, pos, w2)


CMP_TQ = 256
NSA_HALF_BLOCKS = LANES // 2


def _nsa_cmp_kernel(n_slc, slopes_ref, q_ref, kc_ref, vc_ref, ov_ref, oc_ref, sb_ref):
    g = pl.program_id(1)
    i = pl.program_id(2)
    tq = CMP_TQ
    kc = kc_ref[0, 0, 0]
    vc = vc_ref[0, 0, 0]
    nch = kc.shape[0]
    t = i * tq + lax.broadcasted_iota(I32, (tq, nch), 0)
    cend = lax.broadcasted_iota(I32, (tq, nch), 1) * CMP_STRIDE + (CMP_LEN - 1)
    valid = cend <= t
    dist = (t - cend).astype(F32)
    psum = jnp.zeros((tq, nch), F32)
    for r in range(NSA_REP):
        slope = slopes_ref[g * NSA_REP + r]
        q = q_ref[0, :, r * HEAD_DIM:(r + 1) * HEAD_DIM]
        s = jnp.where(valid, _dot_nt(q, kc) * SCALE - slope * dist, NEG)
        m = jnp.max(s, axis=1, keepdims=True)
        m = jnp.where(m > 0.5 * NEG, m, 0.0)
        e = jnp.where(valid, jnp.exp(s - m), 0.0)
        p = e / jnp.maximum(jnp.sum(e, axis=1, keepdims=True), 1e-30)
        oc_ref[0, :, r * HEAD_DIM:(r + 1) * HEAD_DIM] = _dot(p.astype(BF16), vc)
        psum = psum + p
    p_hi = psum.astype(BF16)
    rem = psum - p_hi.astype(F32)
    p_mid = rem.astype(BF16)
    p_lo = (rem - p_mid.astype(F32)).astype(BF16)
    ov = ov_ref[...]
    imp = _dot(p_hi, ov) + _dot(p_mid, ov) + _dot(p_lo, ov)
    lane = lax.broadcasted_iota(I32, (tq, LANES), 1)
    bt = (i * tq + lax.broadcasted_iota(I32, (tq, LANES), 0)) // SLC_BLOCK
    forced = (lane == 0) | (lane == bt) | (lane == bt - 1)
    pri = jnp.where(lane > bt, -jnp.inf, jnp.where(forced, jnp.inf, imp))
    sel = _pick_topk(pri, lane.astype(F32), n_slc, LANES)
    bias = jnp.where(sel, 0.0, NEG)
    low = lane < NSA_HALF_BLOCKS
    sb_ref[0, 0, 0] = jnp.where(low, bias, 0.0).astype(BF16)
    sb_ref[0, 0, 1] = jnp.where(low, pltpu.roll(bias, NSA_HALF_BLOCKS, axis=1), 0.0).astype(BF16)


def _nsa_cmp(proj3, kvc, overlap, slopes):
    b, t, _ = proj3.shape
    nch = kvc.shape[3]
    n_slc = min(SLC_TOPK, t // SLC_BLOCK)
    qblk = COL_NQ // (NSA_REP * HEAD_DIM)
    gw = NSA_REP * HEAD_DIM
    return pl.pallas_call(
        functools.partial(_nsa_cmp_kernel, n_slc),
        grid_spec=pltpu.PrefetchScalarGridSpec(
            num_scalar_prefetch=1,
            grid=(b, NSA_KV_HEADS, t // CMP_TQ),
            in_specs=[
                pl.BlockSpec((1, CMP_TQ, gw), lambda bi, g, i, s: (bi, i, qblk + g)),
                pl.BlockSpec((1, 1, 1, nch, HEAD_DIM), lambda bi, g, i, s: (0, bi, g, 0, 0)),
                pl.BlockSpec((1, 1, 1, nch, HEAD_DIM), lambda bi, g, i, s: (1, bi, g, 0, 0)),
                pl.BlockSpec((nch, LANES), lambda bi, g, i, s: (0, 0)),
            ],
            out_specs=[
                pl.BlockSpec((1, CMP_TQ, gw), lambda bi, g, i, s: (bi, i, g)),
                pl.BlockSpec((1, 1, 2, CMP_TQ, LANES), lambda bi, g, i, s: (bi, g, 0, i, 0)),
            ],
        ),
        out_shape=[jax.ShapeDtypeStruct((b, t, D_NSA), F32),
                   jax.ShapeDtypeStruct((b, NSA_KV_HEADS, 2, t, LANES), BF16)],
        compiler_params=_cparams(("parallel", "parallel", "arbitrary")),
        name="nsa_cmp_attn",
    )(slopes, proj3, kvc, kvc, overlap)


NSA_TQ = 256
SEL_TK = 512
WIN_KEYS = WINDOW + NSA_TQ
HALF_TILES = NSA_HALF_BLOCKS * SLC_BLOCK // SEL_TK


def _nsa_main_kernel(slopes_ref, q_ref, ks_ref, vs_ref, kw_ref, vw_ref, fk_ref, sb_ref, oc_ref, gt_ref, o_ref,
                     s_a, s_b, m_scr, acc_scr):
    g = pl.program_id(1)
    i = pl.program_id(2)
    tq, rows = NSA_TQ, NSA_REP * NSA_TQ
    t0 = i * tq
    tile4 = lambda a: jnp.concatenate([a] * NSA_REP, axis=0)
    qs = jnp.concatenate([q_ref[0, :, r * HEAD_DIM:(r + 1) * HEAD_DIM] for r in range(NSA_REP)], axis=0)
    lane = lax.broadcasted_iota(I32, (rows, LANES), 1)
    slope_raw = jnp.concatenate(
        [jnp.full((tq, LANES), slopes_ref[g * NSA_REP + r] / SCALE, F32) for r in range(NSA_REP)], axis=0)
    alibi = _alibi_cols(slope_raw, lane, NSA_HALF_BLOCKS).astype(BF16)
    qx_lo = jnp.concatenate([qs, tile4(sb_ref[0, 0, 0]) + alibi], axis=1)
    qx_hi = jnp.concatenate([qs, tile4(sb_ref[0, 0, 1]) + alibi], axis=1)
    qx_win = jnp.concatenate([qs, alibi], axis=1)
    ones = jnp.ones((WIN_KEYS, HEAD_DIM), BF16)

    def logits(n):
        start = pl.multiple_of(n * SEL_TK, SEL_TK)
        kx = jnp.concatenate([ks_ref[0, pl.ds(start, SEL_TK), :], fk_ref[pl.ds(start, SEL_TK), :]], axis=1)
        return _dot_nt(jnp.where(n < HALF_TILES, qx_lo, qx_hi), kx)

    def values(n):
        start = pl.multiple_of(n * SEL_TK, SEL_TK)
        return jnp.concatenate([vs_ref[0, pl.ds(start, SEL_TK), :], ones[:SEL_TK]], axis=1)

    last = t0 // SEL_TK
    rq = lax.broadcasted_iota(I32, (tq, SEL_TK), 0)
    ck = lax.broadcasted_iota(I32, (tq, SEL_TK), 1)
    causal = jnp.where(ck - rq <= t0 - last * SEL_TK, 0.0, NEG)
    _causal_flash(logits, values, last, tile4(causal), s_a, s_b, m_scr, acc_scr)
    acc_s = acc_scr[...]

    wstart = pl.multiple_of(jnp.maximum(t0 - WINDOW, 0), NSA_TQ)
    kxw = jnp.concatenate([kw_ref[0, pl.ds(wstart, WIN_KEYS), :], fk_ref[pl.ds(wstart, WIN_KEYS), :]], axis=1)
    vxw = jnp.concatenate([vw_ref[0, pl.ds(wstart, WIN_KEYS), :], ones], axis=1)
    rel = (lax.broadcasted_iota(I32, (tq, WIN_KEYS), 1) + (wstart - t0)
           - lax.broadcasted_iota(I32, (tq, WIN_KEYS), 0))
    wbias = jnp.where(rel <= 0, jnp.where(rel > -WINDOW, 0.0, NEG), NEG)
    s = _dot_nt(qx_win, kxw) + tile4(wbias)
    p = jnp.exp2((s - jnp.max(s, axis=1, keepdims=True)) * EXP2_SCALE)
    acc_w = _dot(p.astype(BF16), vxw)

    o_s = acc_s[:, :HEAD_DIM] / acc_s[:, HEAD_DIM:HEAD_DIM + 1]
    o_w = acc_w[:, :HEAD_DIM] / acc_w[:, HEAD_DIM:HEAD_DIM + 1]
    gt = _sigmoid(gt_ref[0])
    for r in range(NSA_REP):
        dsl = slice(r * HEAD_DIM, (r + 1) * HEAD_DIM)
        rsl = slice(r * tq, (r + 1) * tq)
        o_ref[0, :, dsl] = (gt[:, 3 * r:3 * r + 1] * oc_ref[0, :, dsl]
                            + gt[:, 3 * r + 1:3 * r + 2] * o_s[rsl]
                            + gt[:, 3 * r + 2:3 * r + 3] * o_w[rsl])


def _nsa_main(proj3, selbias, o_c, gates3, slopes):
    b, t, _ = proj3.shape
    gw = NSA_REP * HEAD_DIM
    qblk = COL_NQ // gw
    cb = lambda col: col // HEAD_DIM
    kv_spec = lambda col: pl.BlockSpec((1, t, HEAD_DIM), lambda bi, g, i, s: (bi, 0, cb(col) + g))
    assert t % SEL_TK == 0 and t >= WIN_KEYS and t // SLC_BLOCK <= 2 * NSA_HALF_BLOCKS
    fkey = _key_features(t, SLC_BLOCK, NSA_HALF_BLOCKS, NSA_HALF_BLOCKS)
    return pl.pallas_call(
        _nsa_main_kernel,
        grid_spec=pltpu.PrefetchScalarGridSpec(
            num_scalar_prefetch=1,
            grid=(b, NSA_KV_HEADS, t // NSA_TQ),
            in_specs=[
                pl.BlockSpec((1, NSA_TQ, gw), lambda bi, g, i, s: (bi, i, qblk + g)),
                kv_spec(COL_KS), kv_spec(COL_VS), kv_spec(COL_KW), kv_spec(COL_VW),
                pl.BlockSpec((t, LANES), lambda bi, g, i, s: (0, 0)),
                pl.BlockSpec((1, 1, 2, NSA_TQ, LANES), lambda bi, g, i, s: (bi, g, 0, i, 0)),
                pl.BlockSpec((1, NSA_TQ, gw), lambda bi, g, i, s: (bi, i, g)),
                pl.BlockSpec((1, NSA_TQ, LANES), lambda bi, g, i, s: (bi, i, g)),
            ],
            out_specs=pl.BlockSpec((1, NSA_TQ, gw), lambda bi, g, i, s: (bi, i, g)),
            scratch_shapes=_flash_scratch(NSA_REP * NSA_TQ, SEL_TK),
        ),
        out_shape=jax.ShapeDtypeStruct((b, t, D_NSA), F32),
        compiler_params=_cparams(("parallel", "parallel", "arbitrary")),
        name="nsa_main",
    )(slopes, proj3, proj3, proj3, proj3, proj3, fkey, selbias, o_c, gates3)


def _outproj_kernel(om_ref, on_ref, x_ref, gm_ref, gn_ref, wt_ref, wb_ref, o_ref):
    a = _rms(om_ref[...], gm_ref[...]).astype(BF16)
    c = _rms(on_ref[...], gn_ref[...]).astype(BF16)
    o_ref[...] = x_ref[...] + (_dot(a, wt_ref[...]) + _dot(c, wb_ref[...]))


def _outproj(o_m, o_n, x2, gm, gn, w_top, w_bot):
    n = x2.shape[0]
    tm = min(512, n)
    row = lambda w: pl.BlockSpec((tm, w), lambda i: (i, 0))
    const = lambda r, c: pl.BlockSpec((r, c), lambda i: (0, 0))
    return pl.pallas_call(
        _outproj_kernel,
        grid=(n // tm,),
        in_specs=[row(D_MOBA), row(D_NSA), row(D_MODEL), const(1, D_MOBA), const(1, D_NSA),
                  const(D_MOBA, D_MODEL), const(D_NSA, D_MODEL)],
        out_specs=row(D_MODEL),
        out_shape=jax.ShapeDtypeStruct((n, D_MODEL), F32),
        compiler_params=_cparams(("parallel",)),
        name="outproj",
    )(o_m, o_n, x2, gm, gn, w_top, w_bot)


ROUTER_TM = 512


def _router_kernel(x_ref, g_ref, wr_ref, bias_ref, tri_ref, h_ref, eidx_ref, wgt_ref, pos_ref, cnt_ref, carry):
    step = pl.program_id(0)
    tm = ROUTER_TM

    @pl.when(step == 0)
    def _():
        carry[...] = jnp.zeros_like(carry)

    h = _rms(x_ref[...], g_ref[...])
    h_ref[...] = h
    h_hi, h_lo = _split2(h)
    w_hi, w_lo = _split2(wr_ref[...])
    logits = _dot_nt(w_hi, h_hi) + _dot_nt(w_hi, h_lo) + _dot_nt(w_lo, h_hi)
    aff = _sigmoid(logits)
    biased = aff + bias_ref[...][:, 0:1]

    b3 = biased.reshape(N_GROUPS, GROUP_SIZE, tm)
    sub = lax.broadcasted_iota(I32, b3.shape, 1)
    m1 = jnp.max(b3, axis=1, keepdims=True)
    i1 = jnp.min(jnp.where(b3 == m1, sub, GROUP_SIZE), axis=1, keepdims=True)
    m2 = jnp.max(jnp.where(sub == i1, -jnp.inf, b3), axis=1, keepdims=True)
    gscore = (m1 + m2).reshape(N_GROUPS, tm)

    giota = lax.broadcasted_iota(I32, (N_GROUPS, tm), 0)
    gsel = jnp.zeros((N_GROUPS, tm), jnp.bool_)
    gs = gscore
    for _ in range(TOPK_GROUPS):
        m = jnp.max(gs, axis=0, keepdims=True)
        idx = jnp.min(jnp.where(gs == m, giota, N_GROUPS), axis=0, keepdims=True)
        pick = giota == idx
        gsel = gsel | pick
        gs = jnp.where(pick, -jnp.inf, gs)
    emask = jnp.broadcast_to(jnp.where(gsel, 1.0, 0.0)[:, None, :], (N_GROUPS, GROUP_SIZE, tm)).reshape(N_EXPERTS, tm)

    eiota = lax.broadcasted_iota(I32, (N_EXPERTS, tm), 0)
    cand = jnp.where(emask > 0.5, biased, -jnp.inf)
    multi = jnp.zeros((N_EXPERTS, tm), F32)
    idxs, wts = [], []
    for _ in range(TOP_K):
        m = jnp.max(cand, axis=0, keepdims=True)
        idx = jnp.min(jnp.where(cand == m, eiota, N_EXPERTS), axis=0, keepdims=True)
        pick = eiota == idx
        idxs.append(idx)
        wts.append(jnp.sum(jnp.where(pick, aff, 0.0), axis=0, keepdims=True))
        multi = jnp.where(pick, 1.0, multi)
        cand = jnp.where(pick, -jnp.inf, cand)
    wsum = wts[0]
    for w in wts[1:]:
        wsum = wsum + w

    base = carry[...][:, 0:1]
    cum = _dot(multi.astype(BF16), tri_ref[...]) + base
    poss = [jnp.sum(jnp.where(eiota == idx, cum, 0.0), axis=0, keepdims=True) for idx in idxs]
    total = base + jnp.sum(multi, axis=1, keepdims=True)
    carry[...] = jnp.broadcast_to(total, carry.shape)
    cnt_ref[...] = jnp.broadcast_to(total, cnt_ref.shape)
    eidx_ref[...] = jnp.concatenate(idxs, axis=0)
    wgt_ref[...] = jnp.concatenate([w / wsum * ROUTED_SCALE for w in wts], axis=0)
    pos_ref[...] = jnp.concatenate(poss, axis=0).astype(I32)


def _router(x1, g, wr_t, bias, tri):
    n = x1.shape[0]
    tm = ROUTER_TM
    const = lambda r, c: pl.BlockSpec((r, c), lambda i: (0, 0))
    tok = lambda: pl.BlockSpec((TOP_K, tm), lambda i: (0, i))
    return pl.pallas_call(
        _router_kernel,
        grid=(n // tm,),
        in_specs=[pl.BlockSpec((tm, D_MODEL), lambda i: (i, 0)), const(1, D_MODEL), const(N_EXPERTS, D_MODEL),
                  const(N_EXPERTS, LANES), const(tm, tm)],
        out_specs=[pl.BlockSpec((tm, D_MODEL), lambda i: (i, 0)), tok(), tok(), tok(), const(N_EXPERTS, LANES)],
        out_shape=[jax.ShapeDtypeStruct((n, D_MODEL), F32), jax.ShapeDtypeStruct((TOP_K, n), I32),
                   jax.ShapeDtypeStruct((TOP_K, n), F32), jax.ShapeDtypeStruct((TOP_K, n), I32),
                   jax.ShapeDtypeStruct((N_EXPERTS, LANES), F32)],
        scratch_shapes=[pltpu.VMEM((N_EXPERTS, LANES), F32)],
        compiler_params=_cparams(("arbitrary",)),
        name="router",
    )(x1, g, wr_t, bias, tri)


DISPATCH_TM = 256
ZERO_ROWS = EXPERT_TILE + 8


def _dispatch_kernel(pad_ref, tail_ref, dest_ref, h_ref, xs_hbm, zeros_vmem, sem, zsem):
    step = pl.program_id(0)
    tm = DISPATCH_TM

    @pl.when(step == 0)
    def _():
        zeros_vmem[...] = jnp.zeros_like(zeros_vmem)

        def fill(e, _):
            start = pl.multiple_of((pad_ref[e] // 8) * 8, 8)
            pltpu.make_async_copy(zeros_vmem, xs_hbm.at[pl.ds(start, ZERO_ROWS), :], zsem).start()
            return 0

        lax.fori_loop(0, N_EXPERTS, fill, 0)

        def drain(e, _):
            pltpu.make_async_copy(zeros_vmem, xs_hbm.at[pl.ds(0, ZERO_ROWS), :], zsem).wait()
            return 0

        lax.fori_loop(0, N_EXPERTS, drain, 0)

        tail_start = tail_ref[0]
        n_tail = (xs_hbm.shape[0] - tail_start) // EXPERT_TILE

        def fill_tail(i, _):
            start = pl.multiple_of(tail_start + i * EXPERT_TILE, EXPERT_TILE)
            pltpu.make_async_copy(zeros_vmem.at[pl.ds(0, EXPERT_TILE), :],
                                  xs_hbm.at[pl.ds(start, EXPERT_TILE), :], zsem).start()
            return 0

        lax.fori_loop(0, n_tail, fill_tail, 0)

        def drain_tail(i, _):
            pltpu.make_async_copy(zeros_vmem.at[pl.ds(0, EXPERT_TILE), :],
                                  xs_hbm.at[pl.ds(0, EXPERT_TILE), :], zsem).wait()
            return 0

        lax.fori_loop(0, n_tail, drain_tail, 0)

    def issue(j, _):
        for k in range(TOP_K):
            pltpu.make_async_copy(h_ref.at[pl.ds(j, 1), :],
                                  xs_hbm.at[pl.ds(dest_ref[j * TOP_K + k], 1), :], sem).start()
        return 0

    lax.fori_loop(0, tm, issue, 0)

    def drain(k, _):
        pltpu.make_async_copy(h_ref, xs_hbm.at[pl.ds(0, tm), :], sem).wait()
        return 0

    lax.fori_loop(0, TOP_K, drain, 0)


def _dispatch(h2, dest_flat, pad_start, tail_start, p_alloc):
    n = h2.shape[0]
    tm = min(DISPATCH_TM, n)
    return pl.pallas_call(
        _dispatch_kernel,
        grid_spec=pltpu.PrefetchScalarGridSpec(
            num_scalar_prefetch=2,
            grid=(n // tm,),
            in_specs=[pl.BlockSpec((tm * TOP_K,), lambda i, s, u: (i,), memory_space=pltpu.SMEM),
                      pl.BlockSpec((tm, D_MODEL), lambda i, s, u: (i, 0))],
            out_specs=pl.BlockSpec(memory_space=pl.ANY),
            scratch_shapes=[pltpu.VMEM((ZERO_ROWS, D_MODEL), F32), pltpu.SemaphoreType.DMA,
                            pltpu.SemaphoreType.DMA],
        ),
        out_shape=jax.ShapeDtypeStruct((p_alloc, D_MODEL), F32),
        compiler_params=_cparams(("arbitrary",)),
        name="moe_dispatch",
    )(pad_start, tail_start, dest_flat, h2)


def _moe_kernel(ce_ref, nu_ref, x_ref, wg_ref, wu_ref, wd_ref, y_ref, wg_b, wu_b, wd_b):
    c = pl.program_id(0)

    @pl.when(c < nu_ref[0])
    def _():
        @pl.when((c == 0) | (ce_ref[c] != ce_ref[jnp.maximum(c - 1, 0)]))
        def _():
            wg_b[...] = wg_ref[0].astype(BF16)
            wu_b[...] = wu_ref[0].astype(BF16)
            wd_b[...] = wd_ref[0].astype(BF16)

        x = x_ref[...].astype(BF16)
        gp = _dot(x, wg_b[...])
        up = _dot(x, wu_b[...])
        y_ref[...] = _dot((gp * _sigmoid(gp) * up).astype(BF16), wd_b[...])

    @pl.when(c >= nu_ref[0])
    def _():
        y_ref[...] = jnp.zeros_like(y_ref)


def _moe(xs, chunk_e, n_used, w_gate_e, w_up_e, w_down_e, n_chunks):
    c = EXPERT_TILE
    row = lambda i, ce, nu: (jnp.minimum(i, nu[0] - 1), 0)
    wsel = lambda i, ce, nu: (ce[jnp.minimum(i, nu[0] - 1)], 0, 0)
    return pl.pallas_call(
        _moe_kernel,
        grid_spec=pltpu.PrefetchScalarGridSpec(
            num_scalar_prefetch=2,
            grid=(n_chunks,),
            in_specs=[pl.BlockSpec((c, D_MODEL), row),
                      pl.BlockSpec((1, D_MODEL, D_EXPERT), wsel),
                      pl.BlockSpec((1, D_MODEL, D_EXPERT), wsel),
                      pl.BlockSpec((1, D_EXPERT, D_MODEL), wsel)],
            out_specs=pl.BlockSpec((c, D_MODEL), lambda i, ce, nu: (i, 0)),
            scratch_shapes=[pltpu.VMEM((D_MODEL, D_EXPERT), BF16), pltpu.VMEM((D_MODEL, D_EXPERT), BF16),
                            pltpu.VMEM((D_EXPERT, D_MODEL), BF16)],
        ),
        out_shape=jax.ShapeDtypeStruct((n_chunks * c, D_MODEL), F32),
        compiler_params=_cparams(("arbitrary",)),
        name="moe_experts",
    )(chunk_e, n_used, xs, w_gate_e, w_up_e, w_down_e)


COMBINE_TM = 64


def _combine_kernel(dest_ref, next_ref, w_ref, y_hbm, o_ref, rows, sems):
    step = pl.program_id(0)
    tm = COMBINE_TM
    slot = step % 2

    def gather(idx_ref, to_slot):
        def issue(j, _):
            for k in range(TOP_K):
                pltpu.make_async_copy(y_hbm.at[pl.ds(idx_ref[j * TOP_K + k], 1), :],
                                      rows.at[to_slot, pl.ds(k * tm + j, 1), :], sems.at[to_slot]).start()
            return 0

        lax.fori_loop(0, tm, issue, 0)

    @pl.when(step == 0)
    def _():
        gather(dest_ref, 0)

    @pl.when(step + 1 < pl.num_programs(0))
    def _():
        gather(next_ref, 1 - slot)

    pltpu.make_async_copy(y_hbm.at[pl.ds(0, tm * TOP_K), :], rows.at[slot], sems.at[slot]).wait()
    w = w_ref[...]
    acc = w[:, 0:1] * rows[slot, 0:tm, :]
    for k in range(1, TOP_K):
        acc = acc + w[:, k:k + 1] * rows[slot, k * tm:(k + 1) * tm, :]
    o_ref[...] = acc


def _combine(ys, dest_flat, w_tok):
    n = w_tok.shape[0]
    tm = COMBINE_TM
    steps = n // tm
    return pl.pallas_call(
        _combine_kernel,
        grid=(steps,),
        in_specs=[pl.BlockSpec((tm * TOP_K,), lambda i: (i,), memory_space=pltpu.SMEM),
                  pl.BlockSpec((tm * TOP_K,), lambda i: (jnp.minimum(i + 1, steps - 1),), memory_space=pltpu.SMEM),
                  pl.BlockSpec((tm, TOP_K), lambda i: (i, 0)),
                  pl.BlockSpec(memory_space=pl.ANY)],
        out_specs=pl.BlockSpec((tm, D_MODEL), lambda i: (i, 0)),
        out_shape=jax.ShapeDtypeStruct((n, D_MODEL), F32),
        scratch_shapes=[pltpu.VMEM((2, tm * TOP_K, D_MODEL), F32), pltpu.SemaphoreType.DMA((2,))],
        compiler_params=_cparams(("arbitrary",)),
        name="moe_combine",
    )(dest_flat, dest_flat, w_tok, ys)


def _final_kernel(last, x1_ref, rt_ref, h_ref, p_ref, wg_ref, wu_ref, wd_ref, gp_ref, wple_ref, wpg_ref, gf_ref, o_ref):
    hb = h_ref[...].astype(BF16)
    gp = _dot(hb, wg_ref[...])
    up = _dot(hb, wu_ref[...])
    shared = _dot((gp * _sigmoid(gp) * up).astype(BF16), wd_ref[...])
    x2 = x1_ref[...] + (rt_ref[...] + shared)
    gate = _sigmoid(_dot(_rms(x2, gp_ref[...]).astype(BF16), wpg_ref[...]))
    x3 = x2 + _dot(p_ref[...].astype(BF16), wple_ref[...]) * gate
    o_ref[...] = _rms(x3, gf_ref[...]) if last else x3


def _final(x1, routed, h2, p2, wgs, wus, wds, g_ple, w_ple, w_pg, g_final, last):
    n = x1.shape[0]
    tm = min(128, n)
    row = lambda w: pl.BlockSpec((tm, w), lambda i: (i, 0))
    const = lambda r, c: pl.BlockSpec((r, c), lambda i: (0, 0))
    return pl.pallas_call(
        functools.partial(_final_kernel, last),
        grid=(n // tm,),
        in_specs=[row(D_MODEL), row(D_MODEL), row(D_MODEL), row(PLE_DIM),
                  const(D_MODEL, D_EXPERT), const(D_MODEL, D_EXPERT), const(D_EXPERT, D_MODEL),
                  const(1, D_MODEL), const(PLE_DIM, D_MODEL), const(D_MODEL, D_MODEL), const(1, D_MODEL)],
        out_specs=row(D_MODEL),
        out_shape=jax.ShapeDtypeStruct((n, D_MODEL), F32),
        compiler_params=_cparams(("parallel",)),
        name="shared_ple_final",
    )(x1, routed, h2, p2, wgs, wus, wds, g_ple, w_ple, w_pg, g_final)


def _alibi_slopes(n):
    return jnp.asarray(2.0 ** (-8.0 * np.arange(1, n + 1) / n), dtype=F32)


def _layer(x, p_i, norm_mix_g, w_in, moba_out_g, nsa_out_g, cmp_pos_k, cmp_w1_k, cmp_w2_k,
           cmp_pos_v, cmp_w1_v, cmp_w2_v, w_out, norm_ffn_g, w_router, router_bias,
           w_gate_e, w_up_e, w_down_e, w_gate_s, w_up_s, w_down_s, norm_ple_g, w_ple, w_ple_gate,
           norm_final_g, last):
    b, t, _ = x.shape
    n = b * t
    slopes = _alibi_slopes(N_HEADS_MOBA + N_HEADS_NSA)
    x2 = x.reshape(n, D_MODEL)

    w_main = w_in[:, :D_PROJ].astype(BF16)
    wg = w_in[:, COL_GATE:].reshape(D_MODEL, NSA_KV_HEADS, NSA_REP * 3)
    w_gate = jnp.pad(wg, ((0, 0), (0, 0), (0, LANES - NSA_REP * 3))).reshape(D_MODEL, 2 * LANES).astype(BF16)
    proj, gates = _inproj(x2, norm_mix_g.reshape(1, D_MODEL), w_main, w_gate)
    proj3 = proj.reshape(b, t, D_PROJ)
    gates3 = gates.reshape(b, t, 2 * LANES)

    nb = t // MOBA_BLOCK
    kmean = _kmean(proj3)
    kmean_pad = jnp.pad(kmean, ((0, 0), (0, LANES - nb), (0, 0)))
    o_m = _moba(proj3, kmean_pad, slopes[0::2])

    nch = t // CMP_STRIDE
    kvc = jnp.stack([proj3[:, :, COL_KC:COL_KC + D_KV], proj3[:, :, COL_VC:COL_VC + D_KV]])
    kvc = kvc.reshape(2, b, t, NSA_KV_HEADS, HEAD_DIM).transpose(0, 1, 3, 2, 4)
    kvc = kvc.reshape(2, b, NSA_KV_HEADS, nch, CMP_STRIDE * HEAD_DIM)
    w1 = jnp.stack([cmp_w1_k, cmp_w1_v])
    half = CMP_STRIDE * HEAD_DIM
    w1cat = jnp.concatenate([w1[:, :half], w1[:, half:]], axis=2).astype(BF16)
    pos = jnp.stack([cmp_pos_k, cmp_pos_v]).reshape(2, 1, CMP_LEN * HEAD_DIM)
    pos = jnp.broadcast_to(pos, (2, 8, CMP_LEN * HEAD_DIM))
    w2 = jnp.stack([cmp_w2_k, cmp_w2_v]).astype(BF16)
    kv_cmp = _compress(kvc, w1cat, w1, pos, w2)
    cs = np.arange(nch)[:, None] * CMP_STRIDE
    ss = np.arange(LANES)[None, :] * SLC_BLOCK
    overlap = jnp.asarray(((cs + CMP_LEN - 1 >= ss) & (cs <= ss + SLC_BLOCK - 1)).astype(np.float32), dtype=BF16)
    o_c, selbias = _nsa_cmp(proj3, kv_cmp, overlap, slopes[1::2])
    o_n = _nsa_main(proj3, selbias, o_c, gates3, slopes[1::2])

    w_out_b = w_out.astype(BF16)
    x1 = _outproj(o_m.reshape(n, D_MOBA), o_n.reshape(n, D_NSA), x2, moba_out_g.reshape(1, D_MOBA),
                  nsa_out_g.reshape(1, D_NSA), w_out_b[:D_MOBA], w_out_b[D_MOBA:])

    tri = jnp.asarray(np.triu(np.ones((ROUTER_TM, ROUTER_TM), np.float32), k=1), dtype=BF16)
    bias = jnp.broadcast_to(router_bias.astype(F32)[:, None], (N_EXPERTS, LANES))
    h2, eidx_t, wgt_t, pos_t, cnt = _router(x1, norm_ffn_g.reshape(1, D_MODEL), w_router.T, bias, tri)
    counts = cnt[:, 0].astype(I32)
    c = EXPERT_TILE
    pcounts = (counts + c - 1) // c * c
    pend = jnp.cumsum(pcounts)
    pstart = pend - pcounts
    experts = jnp.arange(N_EXPERTS, dtype=I32)
    seg_start = jnp.sum(jnp.where(eidx_t[..., None] == experts, pstart, 0), axis=-1)
    dest_flat = (seg_start + pos_t).T.reshape(n * TOP_K)
    n_chunks = -(-(n * TOP_K + N_EXPERTS * (c - 1)) // c)
    n_used = (pend[-1] // c).astype(I32).reshape(1)
    chunk_row = jnp.arange(n_chunks, dtype=I32)[:, None] * c
    chunk_e = jnp.minimum(jnp.sum((pend[None, :] <= chunk_row).astype(I32), axis=1), N_EXPERTS - 1)

    xs = _dispatch(h2, dest_flat, (pstart + counts).astype(I32), pend[-1:].astype(I32), (n_chunks + 2) * c)
    ys = _moe(xs, chunk_e, n_used, w_gate_e, w_up_e, w_down_e, n_chunks)
    routed = _combine(ys, dest_flat, wgt_t.T)

    return _final(x1, routed, h2, p_i.reshape(n, PLE_DIM), w_gate_s.astype(BF16), w_up_s.astype(BF16),
                  w_down_s.astype(BF16), norm_ple_g.reshape(1, D_MODEL), w_ple.astype(BF16),
                  w_ple_gate.astype(BF16), norm_final_g.reshape(1, D_MODEL), last).reshape(b, t, D_MODEL)


def kernel(x, p, norm_mix_g, w_in, moba_out_g, nsa_out_g, cmp_pos_k, cmp_w1_k, cmp_w2_k, cmp_pos_v, cmp_w1_v, cmp_w2_v, w_out, norm_ffn_g, w_router, router_bias, w_gate_e, w_up_e, w_down_e, w_gate_s, w_up_s, w_down_s, norm_ple_g, w_ple, w_ple_gate, norm_final_g):
    per_layer = (norm_mix_g, w_in, moba_out_g, nsa_out_g, cmp_pos_k, cmp_w1_k, cmp_w2_k, cmp_pos_v, cmp_w1_v,
                 cmp_w2_v, w_out, norm_ffn_g, w_router, router_bias, w_gate_e, w_up_e, w_down_e,
                 w_gate_s, w_up_s, w_down_s, norm_ple_g, w_ple, w_ple_gate)
    depth = p.shape[0]
    for i in range(depth):
        x = _layer(x, p[i], *(a[i] for a in per_layer), norm_final_g, i == depth - 1)
    return x
```

```python
import functools

import numpy as np
import jax
import jax.numpy as jnp
from jax import lax
from jax.experimental import pallas as pl
from jax.experimental.pallas import tpu as pltpu

F32 = jnp.float32
BF16 = jnp.bfloat16
I32 = jnp.int32

D_MODEL = 2048
HEAD_DIM = 128
N_HEADS_MOBA = 8
N_HEADS_NSA = 8
NSA_KV_HEADS = 2
NSA_REP = N_HEADS_NSA // NSA_KV_HEADS
MOBA_BLOCK = 256
MOBA_TOPK = 3
CMP_LEN = 32
CMP_STRIDE = 16
SLC_BLOCK = 64
SLC_TOPK = 16
WINDOW = 512
N_EXPERTS = 64
N_GROUPS = 8
GROUP_SIZE = N_EXPERTS // N_GROUPS
TOPK_GROUPS = 4
TOP_K = 8
D_EXPERT = 512
ROUTED_SCALE = 2.5
PLE_DIM = 256
RMS_EPS = 1e-6
D_MOBA = N_HEADS_MOBA * HEAD_DIM
D_NSA = N_HEADS_NSA * HEAD_DIM
D_KV = NSA_KV_HEADS * HEAD_DIM
COL_MQ, COL_MK, COL_MV, COL_NQ = 0, D_MOBA, 2 * D_MOBA, 3 * D_MOBA
COL_KC = COL_NQ + D_NSA
COL_VC, COL_KS, COL_VS, COL_KW, COL_VW = (COL_KC + D_KV * i for i in range(1, 6))
D_PROJ = COL_VW + D_KV
COL_GATE = D_PROJ

LANES = 128
V7X_VMEM_LIMIT = 56 * 1024 * 1024
NEG = -1e30

EXPERT_TILE = 512
SCALE = HEAD_DIM ** -0.5
NT = (((1,), (1,)), ((), ()))


def _cparams(sem):
    return pltpu.CompilerParams(dimension_semantics=sem, vmem_limit_bytes=V7X_VMEM_LIMIT)


def _dot(a, b):
    return jnp.dot(a, b, preferred_element_type=F32)


def _dot_nt(a, b):
    return lax.dot_general(a, b, NT, preferred_element_type=F32)


def _split2(a):
    hi = a.astype(BF16)
    lo = (a - hi.astype(F32)).astype(BF16)
    return hi, lo


def _rms(x, g):
    ms = jnp.mean(x * x, axis=-1, keepdims=True)
    return x * lax.rsqrt(ms + RMS_EPS) * g


def _sigmoid(x):
    return 1.0 / (1.0 + jnp.exp(-x))


EXP2_SCALE = SCALE * float(np.log2(np.e))


def _causal_flash(logits, values, n_full, last_bias, s_a, s_b, m_scr, acc_scr):
    def update(s_ref, n, bias=None):
        s = s_ref[...] if bias is None else s_ref[...] + bias
        m = m_scr[...]
        m_new = jnp.maximum(m, jnp.max(s, axis=1, keepdims=True))
        alpha = jnp.exp2((m - m_new) * EXP2_SCALE)
        p = jnp.exp2((s - m_new) * EXP2_SCALE)
        m_scr[...] = m_new
        acc_scr[...] = alpha * acc_scr[...] + _dot(p.astype(BF16), values(n))

    m_scr[...] = jnp.full(m_scr.shape, NEG, F32)
    acc_scr[...] = jnp.zeros(acc_scr.shape, F32)
    s_a[...] = logits(0)

    def pair(k, _):
        s_b[...] = logits(2 * k + 1)
        update(s_a, 2 * k)
        s_a[...] = logits(2 * k + 2)
        update(s_b, 2 * k + 1)
        return 0

    lax.fori_loop(0, n_full // 2, pair, 0)

    @pl.when(n_full % 2 == 1)
    def _():
        s_b[...] = logits(n_full)
        update(s_a, n_full - 1)
        update(s_b, n_full, last_bias)

    @pl.when(n_full % 2 == 0)
    def _():
        update(s_a, n_full, last_bias)


def _alibi_cols(slope_raw, lane, first):
    hi = slope_raw.astype(BF16).astype(F32)
    lo = (slope_raw - hi).astype(BF16).astype(F32)
    is_hi = (lane == first) | (lane == first + 2)
    is_lo = (lane == first + 1) | (lane == first + 3)
    return jnp.where(is_hi, hi, jnp.where(is_lo, lo, 0.0))


def _key_features(t, block, n_lanes_onehot, first):
    pos = np.arange(t)
    f = np.zeros((t, LANES), np.float32)
    f[pos, (pos // block) % n_lanes_onehot] = 1.0
    f[:, first] = f[:, first + 1] = 256 * (pos // 256)
    f[:, first + 2] = f[:, first + 3] = pos % 256
    return jnp.asarray(f, dtype=BF16)


def _pick_topk(score, lane, k, sentinel, axis=1):
    sel = jnp.zeros(score.shape, jnp.bool_)
    g = score
    sentinel = float(sentinel)
    for _ in range(k):
        m = jnp.max(g, axis=axis, keepdims=True)
        idx = jnp.min(jnp.where(g == m, lane, sentinel), axis=axis, keepdims=True)
        idx = jnp.where(m > -jnp.inf, idx, sentinel)
        pick = lane == idx
        sel = sel | pick
        g = jnp.where(pick, -jnp.inf, g)
    return sel


def _inproj_kernel(x_ref, g_ref, w_ref, wg_ref, proj_ref, gate_ref, h_scr):
    @pl.when(pl.program_id(1) == 0)
    def _():
        hb = _rms(x_ref[...], g_ref[...]).astype(BF16)
        h_scr[...] = hb
        gate_ref[...] = _dot(hb, wg_ref[...])

    proj_ref[...] = _dot(h_scr[...], w_ref[...]).astype(BF16)


def _inproj(x2, g, w_main, w_gate):
    n = x2.shape[0]
    tm = min(1024, n)
    tn = 512
    return pl.pallas_call(
        _inproj_kernel,
        grid=(n // tm, D_PROJ // tn),
        in_specs=[
            pl.BlockSpec((tm, D_MODEL), lambda i, j: (i, 0)),
            pl.BlockSpec((1, D_MODEL), lambda i, j: (0, 0)),
            pl.BlockSpec((D_MODEL, tn), lambda i, j: (0, j)),
            pl.BlockSpec((D_MODEL, 2 * LANES), lambda i, j: (0, 0)),
        ],
        out_specs=[
            pl.BlockSpec((tm, tn), lambda i, j: (i, j)),
            pl.BlockSpec((tm, 2 * LANES), lambda i, j: (i, 0)),
        ],
        out_shape=[jax.ShapeDtypeStruct((n, D_PROJ), BF16), jax.ShapeDtypeStruct((n, 2 * LANES), F32)],
        scratch_shapes=[pltpu.VMEM((tm, D_MODEL), BF16)],
        compiler_params=_cparams(("parallel", "arbitrary")),
        name="inproj",
    )(x2, g, w_main, w_gate)


def _kmean_kernel(k_ref, o_ref):
    k = k_ref[0].astype(F32)
    o_ref[0] = jnp.mean(k.reshape(8, MOBA_BLOCK, D_MOBA), axis=1)


def _kmean(proj3):
    b, t, _ = proj3.shape
    nb = t // MOBA_BLOCK
    return pl.pallas_call(
        _kmean_kernel,
        grid=(b, nb // 8),
        in_specs=[pl.BlockSpec((1, 8 * MOBA_BLOCK, D_MOBA), lambda bi, j: (bi, j, COL_MK // D_MOBA))],
        out_specs=pl.BlockSpec((1, 8, D_MOBA), lambda bi, j: (bi, j, 0)),
        out_shape=jax.ShapeDtypeStruct((b, nb, D_MOBA), F32),
        compiler_params=_cparams(("parallel", "parallel")),
        name="moba_kmean",
    )(proj3)


MOBA_TQ = 2 * MOBA_BLOCK
MOBA_TK = 2 * MOBA_BLOCK
MOBA_ALIBI_LANE = 120


def _flash_scratch(rows, tk):
    return [pltpu.VMEM((rows, tk), F32), pltpu.VMEM((rows, tk), F32), pltpu.VMEM((rows, 1), F32),
            pltpu.VMEM((rows, 2 * HEAD_DIM), F32)]


def _moba_kernel(slopes_ref, q_ref, k_ref, v_ref, km_ref, fk_ref, causal_ref, o_ref, s_a, s_b, m_scr, acc_scr):
    h = pl.program_id(1)
    j = pl.program_id(2)
    tq, tk = MOBA_TQ, MOBA_TK
    q = q_ref[0]
    km_hi, km_lo = _split2(km_ref[0])
    gate = _dot_nt(km_hi, q) + _dot_nt(km_lo, q)
    blk = lax.broadcasted_iota(I32, (LANES, tq), 0)
    own = j * (tq // MOBA_BLOCK) + lax.broadcasted_iota(I32, (LANES, tq), 1) // MOBA_BLOCK
    sel = _pick_topk(jnp.where(blk < own, gate, -jnp.inf), blk.astype(F32), MOBA_TOPK, LANES, axis=0)
    sel = sel | (blk == own)
    lane = lax.broadcasted_iota(I32, (tq, LANES), 1)
    slope_raw = jnp.full((tq, LANES), slopes_ref[h] / SCALE, F32)
    extra = jnp.where(lane < MOBA_ALIBI_LANE, jnp.where(sel, 0.0, NEG).T,
                      _alibi_cols(slope_raw, lane, MOBA_ALIBI_LANE))
    qx = jnp.concatenate([q, extra.astype(BF16)], axis=1)
    ones = jnp.ones((tk, HEAD_DIM), BF16)

    def logits(n):
        start = pl.multiple_of(n * tk, tk)
        kx = jnp.concatenate([k_ref[0, pl.ds(start, tk), :], fk_ref[pl.ds(start, tk), :]], axis=1)
        return _dot_nt(qx, kx)

    def values(n):
        start = pl.multiple_of(n * tk, tk)
        return jnp.concatenate([v_ref[0, pl.ds(start, tk), :], ones], axis=1)

    _causal_flash(logits, values, j, causal_ref[...], s_a, s_b, m_scr, acc_scr)
    o_ref[0] = acc_scr[:, :HEAD_DIM] / acc_scr[:, HEAD_DIM:HEAD_DIM + 1]


def _moba(proj3, kmean_pad, slopes):
    b, t, _ = proj3.shape
    assert t // MOBA_BLOCK <= MOBA_ALIBI_LANE and t % MOBA_TQ == 0
    qb, kb, vb = COL_MQ // HEAD_DIM, COL_MK // HEAD_DIM, COL_MV // HEAD_DIM
    fkey = _key_features(t, MOBA_BLOCK, MOBA_ALIBI_LANE, MOBA_ALIBI_LANE)
    causal = jnp.asarray(np.where(np.arange(MOBA_TK)[None, :] <= np.arange(MOBA_TQ)[:, None], 0.0, NEG), dtype=F32)
    return pl.pallas_call(
        _moba_kernel,
        grid_spec=pltpu.PrefetchScalarGridSpec(
            num_scalar_prefetch=1,
            grid=(b, N_HEADS_MOBA, t // MOBA_TQ),
            in_specs=[
                pl.BlockSpec((1, MOBA_TQ, HEAD_DIM), lambda bi, h, j, s: (bi, j, qb + h)),
                pl.BlockSpec((1, t, HEAD_DIM), lambda bi, h, j, s: (bi, 0, kb + h)),
                pl.BlockSpec((1, t, HEAD_DIM), lambda bi, h, j, s: (bi, 0, vb + h)),
                pl.BlockSpec((1, LANES, HEAD_DIM), lambda bi, h, j, s: (bi, 0, h)),
                pl.BlockSpec((t, LANES), lambda bi, h, j, s: (0, 0)),
                pl.BlockSpec((MOBA_TQ, MOBA_TK), lambda bi, h, j, s: (0, 0)),
            ],
            out_specs=pl.BlockSpec((1, MOBA_TQ, HEAD_DIM), lambda bi, h, j, s: (bi, j, h)),
            scratch_shapes=_flash_scratch(MOBA_TQ, MOBA_TK),
        ),
        out_shape=jax.ShapeDtypeStruct((b, t, D_MOBA), F32),
        compiler_params=_cparams(("parallel", "parallel", "arbitrary")),
        name="moba_attn",
    )(slopes, proj3, proj3, proj3, kmean_pad, fkey, causal)


def _gelu_tanh(x):
    c = np.float32(np.sqrt(2.0 / np.pi))
    return x * (0.5 * (1.0 + jnp.tanh(c * (x + 0.044715 * (x * x * x)))))


def _cmp_kernel(x_ref, w1c_ref, w1_ref, pos_ref, w2_ref, o_ref):
    x = x_ref[0, 0, 0]
    nch = x.shape[0]
    ab = _dot(x, w1c_ref[0])
    first = ab[:, :HEAD_DIM]
    second = pltpu.roll(ab[:, HEAD_DIM:], shift=nch - 1, axis=0)
    p_hi, p_lo = _split2(pos_ref[0])
    w_hi, w_lo = _split2(w1_ref[0])
    posc = _dot(p_hi, w_hi) + _dot(p_hi, w_lo) + _dot(p_lo, w_hi)
    hid = _gelu_tanh(first + second + posc[0:1])
    o_ref[0, 0, 0] = _dot(hid.astype(BF16), w2_ref[0]).astype(BF16)


def _compress(xc, w1cat, w1, pos, w2):
    _, b, g, nch, _ = xc.shape
    hid = HEAD_DIM
    return pl.pallas_call(
        _cmp_kernel,
        grid=(2, b, g),
        in_specs=[
            pl.BlockSpec((1, 1, 1, nch, CMP_STRIDE * HEAD_DIM), lambda s, bi, gi: (s, bi, gi, 0, 0)),
            pl.BlockSpec((1, CMP_STRIDE * HEAD_DIM, 2 * hid), lambda s, bi, gi: (s, 0, 0)),
            pl.BlockSpec((1, CMP_LEN * HEAD_DIM, hid), lambda s, bi, gi: (s, 0, 0)),
            pl.BlockSpec((1, 8, CMP_LEN * HEAD_DIM), lambda s, bi, gi: (s, 0, 0)),
            pl.BlockSpec((1, hid, HEAD_DIM), lambda s, bi, gi: (s, 0, 0)),
        ],
        out_specs=pl.BlockSpec((1, 1, 1, nch, HEAD_DIM), lambda s, bi, gi: (s, bi, gi, 0, 0)),
        out_shape=jax.ShapeDtypeStruct((2, b, g, nch, HEAD_DIM), BF16),
        compiler_params=_cparams(("parallel", "parallel", "parallel")),
        name="nsa_compress",
    )(xc, w1cat, w1, pos, w2)


CMP_TQ = 256
NSA_HALF_BLOCKS = LANES // 2


def _nsa_cmp_kernel(n_slc, slopes_ref, q_ref, kc_ref, vct_ref, ovt_ref, oc_ref, sb_ref, oct_scr):
    g = pl.program_id(1)
    i = pl.program_id(2)
    tq = CMP_TQ
    kc = kc_ref[0, 0, 0]
    vct = vct_ref[0, 0]
    nch = kc.shape[0]
    t = i * tq + lax.broadcasted_iota(I32, (nch, tq), 1)
    cend = lax.broadcasted_iota(I32, (nch, tq), 0) * CMP_STRIDE + (CMP_LEN - 1)
    valid = cend <= t
    dist = (t - cend).astype(F32)
    psum = jnp.zeros((nch, tq), F32)
    for r in range(NSA_REP):
        slope = slopes_ref[g * NSA_REP + r]
        q = q_ref[0, :, r * HEAD_DIM:(r + 1) * HEAD_DIM]
        s = jnp.where(valid, _dot_nt(kc, q) * SCALE - slope * dist, NEG)
        m = jnp.max(s, axis=0, keepdims=True)
        m = jnp.where(m > 0.5 * NEG, m, 0.0)
        e = jnp.where(valid, jnp.exp(s - m), 0.0)
        p = e / jnp.maximum(jnp.sum(e, axis=0, keepdims=True), 1e-30)
        oct_scr[...] = _dot(vct, p.astype(BF16))
        oc_ref[0, :, r * HEAD_DIM:(r + 1) * HEAD_DIM] = oct_scr[...].T
        psum = psum + p
    p_hi = psum.astype(BF16)
    rem = psum - p_hi.astype(F32)
    p_mid = rem.astype(BF16)
    p_lo = (rem - p_mid.astype(F32)).astype(BF16)
    ovt = ovt_ref[...]
    imp = _dot(ovt, p_hi) + _dot(ovt, p_mid) + _dot(ovt, p_lo)
    blk = lax.broadcasted_iota(I32, (LANES, tq), 0)
    bt = (i * tq + lax.broadcasted_iota(I32, (LANES, tq), 1)) // SLC_BLOCK
    forced = (blk == 0) | (blk == bt) | (blk == bt - 1)
    pri = jnp.where(blk > bt, -jnp.inf, jnp.where(forced, jnp.inf, imp))
    sel = _pick_topk(pri, blk.astype(F32), n_slc, LANES, axis=0)
    bias = jnp.where(sel, 0.0, NEG).T
    lane = lax.broadcasted_iota(I32, (tq, LANES), 1)
    low = lane < NSA_HALF_BLOCKS
    sb_ref[0, 0, 0] = jnp.where(low, bias, 0.0).astype(BF16)
    sb_ref[0, 0, 1] = jnp.where(low, pltpu.roll(bias, NSA_HALF_BLOCKS, axis=1), 0.0).astype(BF16)


def _nsa_cmp(proj3, kvc, overlap, slopes):
    b, t, _ = proj3.shape
    nch = kvc.shape[3]
    n_slc = min(SLC_TOPK, t // SLC_BLOCK)
    qblk = COL_NQ // (NSA_REP * HEAD_DIM)
    gw = NSA_REP * HEAD_DIM
    return pl.pallas_call(
        functools.partial(_nsa_cmp_kernel, n_slc),
        grid_spec=pltpu.PrefetchScalarGridSpec(
            num_scalar_prefetch=1,
            grid=(b, NSA_KV_HEADS, t // CMP_TQ),
            in_specs=[
                pl.BlockSpec((1, CMP_TQ, gw), lambda bi, g, i, s: (bi, i, qblk + g)),
                pl.BlockSpec((1, 1, 1, nch, HEAD_DIM), lambda bi, g, i, s: (0, bi, g, 0, 0)),
                pl.BlockSpec((1, 1, HEAD_DIM, nch), lambda bi, g, i, s: (bi, g, 0, 0)),
                pl.BlockSpec((LANES, nch), lambda bi, g, i, s: (0, 0)),
            ],
            out_specs=[
                pl.BlockSpec((1, CMP_TQ, gw), lambda bi, g, i, s: (bi, i, g)),
                pl.BlockSpec((1, 1, 2, CMP_TQ, LANES), lambda bi, g, i, s: (bi, g, 0, i, 0)),
            ],
            scratch_shapes=[pltpu.VMEM((HEAD_DIM, CMP_TQ), F32)],
        ),
        out_shape=[jax.ShapeDtypeStruct((b, t, D_NSA), F32),
                   jax.ShapeDtypeStruct((b, NSA_KV_HEADS, 2, t, LANES), BF16)],
        compiler_params=_cparams(("parallel", "parallel", "arbitrary")),
        name="nsa_cmp_attn",
    )(slopes, proj3, kvc, jnp.swapaxes(kvc[1], 2, 3), overlap.T)


NSA_TQ = 256
SEL_TK = 512
WIN_KEYS = WINDOW + NSA_TQ
HALF_TILES = NSA_HALF_BLOCKS * SLC_BLOCK // SEL_TK


def _nsa_main_kernel(slopes_ref, q_ref, ks_ref, vs_ref, kw_ref, vw_ref, fk_ref, sb_ref, oc_ref, gt_ref, o_ref,
                     s_a, s_b, m_scr, acc_scr):
    g = pl.program_id(1)
    i = pl.program_id(2)
    tq, rows = NSA_TQ, NSA_REP * NSA_TQ
    t0 = i * tq
    tile4 = lambda a: jnp.concatenate([a] * NSA_REP, axis=0)
    qs = jnp.concatenate([q_ref[0, :, r * HEAD_DIM:(r + 1) * HEAD_DIM] for r in range(NSA_REP)], axis=0)
    lane = lax.broadcasted_iota(I32, (rows, LANES), 1)
    slope_raw = jnp.concatenate(
        [jnp.full((tq, LANES), slopes_ref[g * NSA_REP + r] / SCALE, F32) for r in range(NSA_REP)], axis=0)
    alibi = _alibi_cols(slope_raw, lane, NSA_HALF_BLOCKS).astype(BF16)
    qx_lo = jnp.concatenate([qs, tile4(sb_ref[0, 0, 0]) + alibi], axis=1)
    qx_hi = jnp.concatenate([qs, tile4(sb_ref[0, 0, 1]) + alibi], axis=1)
    qx_win = jnp.concatenate([qs, alibi], axis=1)
    ones = jnp.ones((WIN_KEYS, HEAD_DIM), BF16)

    def logits(n):
        start = pl.multiple_of(n * SEL_TK, SEL_TK)
        kx = jnp.concatenate([ks_ref[0, pl.ds(start, SEL_TK), :], fk_ref[pl.ds(start, SEL_TK), :]], axis=1)
        return _dot_nt(jnp.where(n < HALF_TILES, qx_lo, qx_hi), kx)

    def values(n):
        start = pl.multiple_of(n * SEL_TK, SEL_TK)
        return jnp.concatenate([vs_ref[0, pl.ds(start, SEL_TK), :], ones[:SEL_TK]], axis=1)

    last = t0 // SEL_TK
    rq = lax.broadcasted_iota(I32, (tq, SEL_TK), 0)
    ck = lax.broadcasted_iota(I32, (tq, SEL_TK), 1)
    causal = jnp.where(ck - rq <= t0 - last * SEL_TK, 0.0, NEG)
    _causal_flash(logits, values, last, tile4(causal), s_a, s_b, m_scr, acc_scr)
    acc_s = acc_scr[...]

    wstart = pl.multiple_of(jnp.maximum(t0 - WINDOW, 0), NSA_TQ)
    kxw = jnp.concatenate([kw_ref[0, pl.ds(wstart, WIN_KEYS), :], fk_ref[pl.ds(wstart, WIN_KEYS), :]], axis=1)
    vxw = jnp.concatenate([vw_ref[0, pl.ds(wstart, WIN_KEYS), :], ones], axis=1)
    rel = (lax.broadcasted_iota(I32, (tq, WIN_KEYS), 1) + (wstart - t0)
           - lax.broadcasted_iota(I32, (tq, WIN_KEYS), 0))
    wbias = jnp.where(rel <= 0, jnp.where(rel > -WINDOW, 0.0, NEG), NEG)
    s = _dot_nt(qx_win, kxw) + tile4(wbias)
    p = jnp.exp2((s - jnp.max(s, axis=1, keepdims=True)) * EXP2_SCALE)
    acc_w = _dot(p.astype(BF16), vxw)

    o_s = acc_s[:, :HEAD_DIM] / acc_s[:, HEAD_DIM:HEAD_DIM + 1]
    o_w = acc_w[:, :HEAD_DIM] / acc_w[:, HEAD_DIM:HEAD_DIM + 1]
    gt = _sigmoid(gt_ref[0])
    for r in range(NSA_REP):
        dsl = slice(r * HEAD_DIM, (r + 1) * HEAD_DIM)
        rsl = slice(r * tq, (r + 1) * tq)
        o_ref[0, :, dsl] = (gt[:, 3 * r:3 * r + 1] * oc_ref[0, :, dsl]
                            + gt[:, 3 * r + 1:3 * r + 2] * o_s[rsl]
                            + gt[:, 3 * r + 2:3 * r + 3] * o_w[rsl])


def _nsa_main(proj3, selbias, o_c, gates3, slopes):
    b, t, _ = proj3.shape
    gw = NSA_REP * HEAD_DIM
    qblk = COL_NQ // gw
    cb = lambda col: col // HEAD_DIM
    kv_spec = lambda col: pl.BlockSpec((1, t, HEAD_DIM), lambda bi, g, i, s: (bi, 0, cb(col) + g))
    assert t % SEL_TK == 0 and t >= WIN_KEYS and t // SLC_BLOCK <= 2 * NSA_HALF_BLOCKS
    fkey = _key_features(t, SLC_BLOCK, NSA_HALF_BLOCKS, NSA_HALF_BLOCKS)
    return pl.pallas_call(
        _nsa_main_kernel,
        grid_spec=pltpu.PrefetchScalarGridSpec(
            num_scalar_prefetch=1,
            grid=(b, NSA_KV_HEADS, t // NSA_TQ),
            in_specs=[
                pl.BlockSpec((1, NSA_TQ, gw), lambda bi, g, i, s: (bi, i, qblk + g)),
                kv_spec(COL_KS), kv_spec(COL_VS), kv_spec(COL_KW), kv_spec(COL_VW),
                pl.BlockSpec((t, LANES), lambda bi, g, i, s: (0, 0)),
                pl.BlockSpec((1, 1, 2, NSA_TQ, LANES), lambda bi, g, i, s: (bi, g, 0, i, 0)),
                pl.BlockSpec((1, NSA_TQ, gw), lambda bi, g, i, s: (bi, i, g)),
                pl.BlockSpec((1, NSA_TQ, LANES), lambda bi, g, i, s: (bi, i, g)),
            ],
            out_specs=pl.BlockSpec((1, NSA_TQ, gw), lambda bi, g, i, s: (bi, i, g)),
            scratch_shapes=_flash_scratch(NSA_REP * NSA_TQ, SEL_TK),
        ),
        out_shape=jax.ShapeDtypeStruct((b, t, D_NSA), F32),
        compiler_params=_cparams(("parallel", "parallel", "arbitrary")),
        name="nsa_main",
    )(slopes, proj3, proj3, proj3, proj3, proj3, fkey, selbias, o_c, gates3)


def _outproj_kernel(om_ref, on_ref, x_ref, gm_ref, gn_ref, wt_ref, wb_ref, o_ref):
    a = _rms(om_ref[...], gm_ref[...]).astype(BF16)
    c = _rms(on_ref[...], gn_ref[...]).astype(BF16)
    o_ref[...] = x_ref[...] + (_dot(a, wt_ref[...]) + _dot(c, wb_ref[...]))


def _outproj(o_m, o_n, x2, gm, gn, w_top, w_bot):
    n = x2.shape[0]
    tm = min(512, n)
    row = lambda w: pl.BlockSpec((tm, w), lambda i: (i, 0))
    const = lambda r, c: pl.BlockSpec((r, c), lambda i: (0, 0))
    return pl.pallas_call(
        _outproj_kernel,
        grid=(n // tm,),
        in_specs=[row(D_MOBA), row(D_NSA), row(D_MODEL), const(1, D_MOBA), const(1, D_NSA),
                  const(D_MOBA, D_MODEL), const(D_NSA, D_MODEL)],
        out_specs=row(D_MODEL),
        out_shape=jax.ShapeDtypeStruct((n, D_MODEL), F32),
        compiler_params=_cparams(("parallel",)),
        name="outproj",
    )(o_m, o_n, x2, gm, gn, w_top, w_bot)


ROUTER_TM = 512


def _router_kernel(x_ref, g_ref, wr_ref, bias_ref, tri_ref, h_ref, eidx_ref, wgt_ref, pos_ref, cnt_ref, carry):
    step = pl.program_id(0)
    tm = ROUTER_TM

    @pl.when(step == 0)
    def _():
        carry[...] = jnp.zeros_like(carry)

    h = _rms(x_ref[...], g_ref[...])
    h_ref[...] = h
    h_hi, h_lo = _split2(h)
    w_hi, w_lo = _split2(wr_ref[...])
    logits = _dot_nt(w_hi, h_hi) + _dot_nt(w_hi, h_lo) + _dot_nt(w_lo, h_hi)
    aff = _sigmoid(logits)
    biased = aff + bias_ref[...][:, 0:1]

    b3 = biased.reshape(N_GROUPS, GROUP_SIZE, tm)
    sub = lax.broadcasted_iota(I32, b3.shape, 1)
    m1 = jnp.max(b3, axis=1, keepdims=True)
    i1 = jnp.min(jnp.where(b3 == m1, sub, GROUP_SIZE), axis=1, keepdims=True)
    m2 = jnp.max(jnp.where(sub == i1, -jnp.inf, b3), axis=1, keepdims=True)
    gscore = (m1 + m2).reshape(N_GROUPS, tm)

    giota = lax.broadcasted_iota(I32, (N_GROUPS, tm), 0)
    gsel = jnp.zeros((N_GROUPS, tm), jnp.bool_)
    gs = gscore
    for _ in range(TOPK_GROUPS):
        m = jnp.max(gs, axis=0, keepdims=True)
        idx = jnp.min(jnp.where(gs == m, giota, N_GROUPS), axis=0, keepdims=True)
        pick = giota == idx
        gsel = gsel | pick
        gs = jnp.where(pick, -jnp.inf, gs)
    emask = jnp.broadcast_to(jnp.where(gsel, 1.0, 0.0)[:, None, :], (N_GROUPS, GROUP_SIZE, tm)).reshape(N_EXPERTS, tm)

    eiota = lax.broadcasted_iota(I32, (N_EXPERTS, tm), 0)
    cand = jnp.where(emask > 0.5, biased, -jnp.inf)
    multi = jnp.zeros((N_EXPERTS, tm), F32)
    idxs, wts = [], []
    for _ in range(TOP_K):
        m = jnp.max(cand, axis=0, keepdims=True)
        idx = jnp.min(jnp.where(cand == m, eiota, N_EXPERTS), axis=0, keepdims=True)
        pick = eiota == idx
        idxs.append(idx)
        wts.append(jnp.sum(jnp.where(pick, aff, 0.0), axis=0, keepdims=True))
        multi = jnp.where(pick, 1.0, multi)
        cand = jnp.where(pick, -jnp.inf, cand)
    wsum = wts[0]
    for w in wts[1:]:
        wsum = wsum + w

    base = carry[...][:, 0:1]
    cum = _dot(multi.astype(BF16), tri_ref[...]) + base
    poss = [jnp.sum(jnp.where(eiota == idx, cum, 0.0), axis=0, keepdims=True) for idx in idxs]
    total = base + jnp.sum(multi, axis=1, keepdims=True)
    carry[...] = jnp.broadcast_to(total, carry.shape)
    cnt_ref[...] = jnp.broadcast_to(total, cnt_ref.shape)
    eidx_ref[...] = jnp.concatenate(idxs, axis=0)
    wgt_ref[...] = jnp.concatenate([w / wsum * ROUTED_SCALE for w in wts], axis=0)
    pos_ref[...] = jnp.concatenate(poss, axis=0).astype(I32)


def _router(x1, g, wr_t, bias, tri):
    n = x1.shape[0]
    tm = ROUTER_TM
    const = lambda r, c: pl.BlockSpec((r, c), lambda i: (0, 0))
    tok = lambda: pl.BlockSpec((TOP_K, tm), lambda i: (0, i))
    return pl.pallas_call(
        _router_kernel,
        grid=(n // tm,),
        in_specs=[pl.BlockSpec((tm, D_MODEL), lambda i: (i, 0)), const(1, D_MODEL), const(N_EXPERTS, D_MODEL),
                  const(N_EXPERTS, LANES), const(tm, tm)],
        out_specs=[pl.BlockSpec((tm, D_MODEL), lambda i: (i, 0)), tok(), tok(), tok(), const(N_EXPERTS, LANES)],
        out_shape=[jax.ShapeDtypeStruct((n, D_MODEL), F32), jax.ShapeDtypeStruct((TOP_K, n), I32),
                   jax.ShapeDtypeStruct((TOP_K, n), F32), jax.ShapeDtypeStruct((TOP_K, n), I32),
                   jax.ShapeDtypeStruct((N_EXPERTS, LANES), F32)],
        scratch_shapes=[pltpu.VMEM((N_EXPERTS, LANES), F32)],
        compiler_params=_cparams(("arbitrary",)),
        name="router",
    )(x1, g, wr_t, bias, tri)


DISPATCH_TM = 256
ZERO_ROWS = EXPERT_TILE + 8


def _dispatch_kernel(pad_ref, tail_ref, dest_ref, h_ref, xs_hbm, zeros_vmem, sem, zsem):
    step = pl.program_id(0)
    tm = DISPATCH_TM

    @pl.when(step == 0)
    def _():
        zeros_vmem[...] = jnp.zeros_like(zeros_vmem)

        def fill(e, _):
            start = pl.multiple_of((pad_ref[e] // 8) * 8, 8)
            pltpu.make_async_copy(zeros_vmem, xs_hbm.at[pl.ds(start, ZERO_ROWS), :], zsem).start()
            return 0

        lax.fori_loop(0, N_EXPERTS, fill, 0)

        def drain(e, _):
            pltpu.make_async_copy(zeros_vmem, xs_hbm.at[pl.ds(0, ZERO_ROWS), :], zsem).wait()
            return 0

        lax.fori_loop(0, N_EXPERTS, drain, 0)

        tail_start = tail_ref[0]
        n_tail = (xs_hbm.shape[0] - tail_start) // EXPERT_TILE

        def fill_tail(i, _):
            start = pl.multiple_of(tail_start + i * EXPERT_TILE, EXPERT_TILE)
            pltpu.make_async_copy(zeros_vmem.at[pl.ds(0, EXPERT_TILE), :],
                                  xs_hbm.at[pl.ds(start, EXPERT_TILE), :], zsem).start()
            return 0

        lax.fori_loop(0, n_tail, fill_tail, 0)

        def drain_tail(i, _):
            pltpu.make_async_copy(zeros_vmem.at[pl.ds(0, EXPERT_TILE), :],
                                  xs_hbm.at[pl.ds(0, EXPERT_TILE), :], zsem).wait()
            return 0

        lax.fori_loop(0, n_tail, drain_tail, 0)

    def issue(j, _):
        for k in range(TOP_K):
            pltpu.make_async_copy(h_ref.at[pl.ds(j, 1), :],
                                  xs_hbm.at[pl.ds(dest_ref[j * TOP_K + k], 1), :], sem).start()
        return 0

    lax.fori_loop(0, tm, issue, 0)

    def drain(k, _):
        pltpu.make_async_copy(h_ref, xs_hbm.at[pl.ds(0, tm), :], sem).wait()
        return 0

    lax.fori_loop(0, TOP_K, drain, 0)


def _dispatch(h2, dest_flat, pad_start, tail_start, p_alloc):
    n = h2.shape[0]
    tm = min(DISPATCH_TM, n)
    return pl.pallas_call(
        _dispatch_kernel,
        grid_spec=pltpu.PrefetchScalarGridSpec(
            num_scalar_prefetch=2,
            grid=(n // tm,),
            in_specs=[pl.BlockSpec((tm * TOP_K,), lambda i, s, u: (i,), memory_space=pltpu.SMEM),
                      pl.BlockSpec((tm, D_MODEL), lambda i, s, u: (i, 0))],
            out_specs=pl.BlockSpec(memory_space=pl.ANY),
            scratch_shapes=[pltpu.VMEM((ZERO_ROWS, D_MODEL), F32), pltpu.SemaphoreType.DMA,
                            pltpu.SemaphoreType.DMA],
        ),
        out_shape=jax.ShapeDtypeStruct((p_alloc, D_MODEL), F32),
        compiler_params=_cparams(("arbitrary",)),
        name="moe_dispatch",
    )(pad_start, tail_start, dest_flat, h2)


def _moe_kernel(ce_ref, nu_ref, x_ref, wg_ref, wu_ref, wd_ref, y_ref, wg_b, wu_b, wd_b):
    c = pl.program_id(0)

    @pl.when(c < nu_ref[0])
    def _():
        @pl.when((c == 0) | (ce_ref[c] != ce_ref[jnp.maximum(c - 1, 0)]))
        def _():
            wg_b[...] = wg_ref[0].astype(BF16)
            wu_b[...] = wu_ref[0].astype(BF16)
            wd_b[...] = wd_ref[0].astype(BF16)

        x = x_ref[...].astype(BF16)
        gp = _dot(x, wg_b[...])
        up = _dot(x, wu_b[...])
        y_ref[...] = _dot((gp * _sigmoid(gp) * up).astype(BF16), wd_b[...])

    @pl.when(c >= nu_ref[0])
    def _():
        y_ref[...] = jnp.zeros_like(y_ref)


def _moe(xs, chunk_e, n_used, w_gate_e, w_up_e, w_down_e, n_chunks):
    c = EXPERT_TILE
    row = lambda i, ce, nu: (jnp.minimum(i, nu[0] - 1), 0)
    wsel = lambda i, ce, nu: (ce[jnp.minimum(i, nu[0] - 1)], 0, 0)
    return pl.pallas_call(
        _moe_kernel,
        grid_spec=pltpu.PrefetchScalarGridSpec(
            num_scalar_prefetch=2,
            grid=(n_chunks,),
            in_specs=[pl.BlockSpec((c, D_MODEL), row),
                      pl.BlockSpec((1, D_MODEL, D_EXPERT), wsel),
                      pl.BlockSpec((1, D_MODEL, D_EXPERT), wsel),
                      pl.BlockSpec((1, D_EXPERT, D_MODEL), wsel)],
            out_specs=pl.BlockSpec((c, D_MODEL), lambda i, ce, nu: (i, 0)),
            scratch_shapes=[pltpu.VMEM((D_MODEL, D_EXPERT), BF16), pltpu.VMEM((D_MODEL, D_EXPERT), BF16),
                            pltpu.VMEM((D_EXPERT, D_MODEL), BF16)],
        ),
        out_shape=jax.ShapeDtypeStruct((n_chunks * c, D_MODEL), F32),
        compiler_params=_cparams(("arbitrary",)),
        name="moe_experts",
    )(chunk_e, n_used, xs, w_gate_e, w_up_e, w_down_e)


COMBINE_TM = 64


def _combine_kernel(dest_ref, next_ref, w_ref, y_hbm, o_ref, rows, sems):
    step = pl.program_id(0)
    tm = COMBINE_TM
    slot = step % 2

    def gather(idx_ref, to_slot):
        def issue(j, _):
            for k in range(TOP_K):
                pltpu.make_async_copy(y_hbm.at[pl.ds(idx_ref[j * TOP_K + k], 1), :],
                                      rows.at[to_slot, pl.ds(k * tm + j, 1), :], sems.at[to_slot]).start()
            return 0

        lax.fori_loop(0, tm, issue, 0)

    @pl.when(step == 0)
    def _():
        gather(dest_ref, 0)

    @pl.when(step + 1 < pl.num_programs(0))
    def _():
        gather(next_ref, 1 - slot)

    pltpu.make_async_copy(y_hbm.at[pl.ds(0, tm * TOP_K), :], rows.at[slot], sems.at[slot]).wait()
    w = w_ref[...]
    acc = w[:, 0:1] * rows[slot, 0:tm, :]
    for k in range(1, TOP_K):
        acc = acc + w[:, k:k + 1] * rows[slot, k * tm:(k + 1) * tm, :]
    o_ref[...] = acc


def _combine(ys, dest_flat, w_tok):
    n = w_tok.shape[0]
    tm = COMBINE_TM
    steps = n // tm
    return pl.pallas_call(
        _combine_kernel,
        grid=(steps,),
        in_specs=[pl.BlockSpec((tm * TOP_K,), lambda i: (i,), memory_space=pltpu.SMEM),
                  pl.BlockSpec((tm * TOP_K,), lambda i: (jnp.minimum(i + 1, steps - 1),), memory_space=pltpu.SMEM),
                  pl.BlockSpec((tm, TOP_K), lambda i: (i, 0)),
                  pl.BlockSpec(memory_space=pl.ANY)],
        out_specs=pl.BlockSpec((tm, D_MODEL), lambda i: (i, 0)),
        out_shape=jax.ShapeDtypeStruct((n, D_MODEL), F32),
        scratch_shapes=[pltpu.VMEM((2, tm * TOP_K, D_MODEL), F32), pltpu.SemaphoreType.DMA((2,))],
        compiler_params=_cparams(("arbitrary",)),
        name="moe_combine",
    )(dest_flat, dest_flat, w_tok, ys)


def _final_kernel(last, x1_ref, rt_ref, h_ref, p_ref, wg_ref, wu_ref, wd_ref, gp_ref, wple_ref, wpg_ref, gf_ref, o_ref):
    hb = h_ref[...].astype(BF16)
    gp = _dot(hb, wg_ref[...])
    up = _dot(hb, wu_ref[...])
    shared = _dot((gp * _sigmoid(gp) * up).astype(BF16), wd_ref[...])
    x2 = x1_ref[...] + (rt_ref[...] + shared)
    gate = _sigmoid(_dot(_rms(x2, gp_ref[...]).astype(BF16), wpg_ref[...]))
    x3 = x2 + _dot(p_ref[...].astype(BF16), wple_ref[...]) * gate
    o_ref[...] = _rms(x3, gf_ref[...]) if last else x3


def _final(x1, routed, h2, p2, wgs, wus, wds, g_ple, w_ple, w_pg, g_final, last):
    n = x1.shape[0]
    tm = min(256, n)
    row = lambda w: pl.BlockSpec((tm, w), lambda i: (i, 0))
    const = lambda r, c: pl.BlockSpec((r, c), lambda i: (0, 0))
    return pl.pallas_call(
        functools.partial(_final_kernel, last),
        grid=(n // tm,),
        in_specs=[row(D_MODEL), row(D_MODEL), row(D_MODEL), row(PLE_DIM),
                  const(D_MODEL, D_EXPERT), const(D_MODEL, D_EXPERT), const(D_EXPERT, D_MODEL),
                  const(1, D_MODEL), const(PLE_DIM, D_MODEL), const(D_MODEL, D_MODEL), const(1, D_MODEL)],
        out_specs=row(D_MODEL),
        out_shape=jax.ShapeDtypeStruct((n, D_MODEL), F32),
        compiler_params=_cparams(("parallel",)),
        name="shared_ple_final",
    )(x1, routed, h2, p2, wgs, wus, wds, g_ple, w_ple, w_pg, g_final)


def _alibi_slopes(n):
    return jnp.asarray(2.0 ** (-8.0 * np.arange(1, n + 1) / n), dtype=F32)


def _layer(x, p_i, norm_mix_g, w_in, moba_out_g, nsa_out_g, cmp_pos_k, cmp_w1_k, cmp_w2_k,
           cmp_pos_v, cmp_w1_v, cmp_w2_v, w_out, norm_ffn_g, w_router, router_bias,
           w_gate_e, w_up_e, w_down_e, w_gate_s, w_up_s, w_down_s, norm_ple_g, w_ple, w_ple_gate,
           norm_final_g, last):
    b, t, _ = x.shape
    n = b * t
    slopes = _alibi_slopes(N_HEADS_MOBA + N_HEADS_NSA)
    x2 = x.reshape(n, D_MODEL)

    w_main = w_in[:, :D_PROJ].astype(BF16)
    wg = w_in[:, COL_GATE:].reshape(D_MODEL, NSA_KV_HEADS, NSA_REP * 3)
    w_gate = jnp.pad(wg, ((0, 0), (0, 0), (0, LANES - NSA_REP * 3))).reshape(D_MODEL, 2 * LANES).astype(BF16)
    proj, gates = _inproj(x2, norm_mix_g.reshape(1, D_MODEL), w_main, w_gate)
    proj3 = proj.reshape(b, t, D_PROJ)
    gates3 = gates.reshape(b, t, 2 * LANES)

    nb = t // MOBA_BLOCK
    kmean = _kmean(proj3)
    kmean_pad = jnp.pad(kmean, ((0, 0), (0, LANES - nb), (0, 0)))
    o_m = _moba(proj3, kmean_pad, slopes[0::2])

    nch = t // CMP_STRIDE
    kvc = jnp.stack([proj3[:, :, COL_KC:COL_KC + D_KV], proj3[:, :, COL_VC:COL_VC + D_KV]])
    kvc = kvc.reshape(2, b, t, NSA_KV_HEADS, HEAD_DIM).transpose(0, 1, 3, 2, 4)
    kvc = kvc.reshape(2, b, NSA_KV_HEADS, nch, CMP_STRIDE * HEAD_DIM)
    w1 = jnp.stack([cmp_w1_k, cmp_w1_v])
    half = CMP_STRIDE * HEAD_DIM
    w1cat = jnp.concatenate([w1[:, :half], w1[:, half:]], axis=2).astype(BF16)
    pos = jnp.stack([cmp_pos_k, cmp_pos_v]).reshape(2, 1, CMP_LEN * HEAD_DIM)
    pos = jnp.broadcast_to(pos, (2, 8, CMP_LEN * HEAD_DIM))
    w2 = jnp.stack([cmp_w2_k, cmp_w2_v]).astype(BF16)
    kv_cmp = _compress(kvc, w1cat, w1, pos, w2)
    cs = np.arange(nch)[:, None] * CMP_STRIDE
    ss = np.arange(LANES)[None, :] * SLC_BLOCK
    overlap = jnp.asarray(((cs + CMP_LEN - 1 >= ss) & (cs <= ss + SLC_BLOCK - 1)).astype(np.float32), dtype=BF16)
    o_c, selbias = _nsa_cmp(proj3, kv_cmp, overlap, slopes[1::2])
    o_n = _nsa_main(proj3, selbias, o_c, gates3, slopes[1::2])

    w_out_b = w_out.astype(BF16)
    x1 = _outproj(o_m.reshape(n, D_MOBA), o_n.reshape(n, D_NSA), x2, moba_out_g.reshape(1, D_MOBA),
                  nsa_out_g.reshape(1, D_NSA), w_out_b[:D_MOBA], w_out_b[D_MOBA:])

    tri = jnp.asarray(np.triu(np.ones((ROUTER_TM, ROUTER_TM), np.float32), k=1), dtype=BF16)
    bias = jnp.broadcast_to(router_bias.astype(F32)[:, None], (N_EXPERTS, LANES))
    h2, eidx_t, wgt_t, pos_t, cnt = _router(x1, norm_ffn_g.reshape(1, D_MODEL), w_router.T, bias, tri)
    counts = cnt[:, 0].astype(I32)
    c = EXPERT_TILE
    pcounts = (counts + c - 1) // c * c
    pend = jnp.cumsum(pcounts)
    pstart = pend - pcounts
    experts = jnp.arange(N_EXPERTS, dtype=I32)
    seg_start = jnp.sum(jnp.where(eidx_t[..., None] == experts, pstart, 0), axis=-1)
    dest_flat = (seg_start + pos_t).T.reshape(n * TOP_K)
    n_chunks = -(-(n * TOP_K + N_EXPERTS * (c - 1)) // c)
    n_used = (pend[-1] // c).astype(I32).reshape(1)
    chunk_row = jnp.arange(n_chunks, dtype=I32)[:, None] * c
    chunk_e = jnp.minimum(jnp.sum((pend[None, :] <= chunk_row).astype(I32), axis=1), N_EXPERTS - 1)

    xs = _dispatch(h2, dest_flat, (pstart + counts).astype(I32), pend[-1:].astype(I32), (n_chunks + 2) * c)
    ys = _moe(xs, chunk_e, n_used, w_gate_e, w_up_e, w_down_e, n_chunks)
    routed = _combine(ys, dest_flat, wgt_t.T)

    return _final(x1, routed, h2, p_i.reshape(n, PLE_DIM), w_gate_s.astype(BF16), w_up_s.astype(BF16),
                  w_down_s.astype(BF16), norm_ple_g.reshape(1, D_MODEL), w_ple.astype(BF16),
                  w_ple_gate.astype(BF16), norm_final_g.reshape(1, D_MODEL), last).reshape(b, t, D_MODEL)


def kernel(x, p, norm_mix_g, w_in, moba_out_g, nsa_out_g, cmp_pos_k, cmp_w1_k, cmp_w2_k, cmp_pos_v, cmp_w1_v, cmp_w2_v, w_out, norm_ffn_g, w_router, router_bias, w_gate_e, w_up_e, w_down_e, w_gate_s, w_up_s, w_down_s, norm_ple_g, w_ple, w_ple_gate, norm_final_g):
    per_layer = (norm_mix_g, w_in, moba_out_g, nsa_out_g, cmp_pos_k, cmp_w1_k, cmp_w2_k, cmp_pos_v, cmp_w1_v,
                 cmp_w2_v, w_out, norm_ffn_g, w_router, router_bias, w_gate_e, w_up_e, w_down_e,
                 w_gate_s, w_up_s, w_down_s, norm_ple_g, w_ple, w_ple_gate)
    depth = p.shape[0]
    for i in range(depth):
        x = _layer(x, p[i], *(a[i] for a in per_layer), norm_final_g, i == depth - 1)
    return x
```

```python
import functools

import numpy as np
import jax
import jax.numpy as jnp
from jax import lax
from jax.experimental import pallas as pl
from jax.experimental.pallas import tpu as pltpu

F32 = jnp.float32
BF16 = jnp.bfloat16
I32 = jnp.int32

D_MODEL = 2048
HEAD_DIM = 128
N_HEADS_MOBA = 8
N_HEADS_NSA = 8
NSA_KV_HEADS = 2
NSA_REP = N_HEADS_NSA // NSA_KV_HEADS
MOBA_BLOCK = 256
MOBA_TOPK = 3
CMP_LEN = 32
CMP_STRIDE = 16
SLC_BLOCK = 64
SLC_TOPK = 16
WINDOW = 512
N_EXPERTS = 64
N_GROUPS = 8
GROUP_SIZE = N_EXPERTS // N_GROUPS
TOPK_GROUPS = 4
TOP_K = 8
D_EXPERT = 512
ROUTED_SCALE = 2.5
PLE_DIM = 256
RMS_EPS = 1e-6
D_MOBA = N_HEADS_MOBA * HEAD_DIM
D_NSA = N_HEADS_NSA * HEAD_DIM
D_KV = NSA_KV_HEADS * HEAD_DIM
COL_MQ, COL_MK, COL_MV, COL_NQ = 0, D_MOBA, 2 * D_MOBA, 3 * D_MOBA
COL_KC = COL_NQ + D_NSA
COL_VC, COL_KS, COL_VS, COL_KW, COL_VW = (COL_KC + D_KV * i for i in range(1, 6))
D_PROJ = COL_VW + D_KV
COL_GATE = D_PROJ

LANES = 128
V7X_VMEM_LIMIT = 56 * 1024 * 1024
NEG = -1e30

EXPERT_TILE = 512
SCALE = HEAD_DIM ** -0.5
NT = (((1,), (1,)), ((), ()))


def _cparams(sem):
    return pltpu.CompilerParams(dimension_semantics=sem, vmem_limit_bytes=V7X_VMEM_LIMIT)


def _dot(a, b):
    return jnp.dot(a, b, preferred_element_type=F32)


def _dot_nt(a, b):
    return lax.dot_general(a, b, NT, preferred_element_type=F32)


def _split2(a):
    hi = a.astype(BF16)
    lo = (a - hi.astype(F32)).astype(BF16)
    return hi, lo


def _rms(x, g):
    ms = jnp.mean(x * x, axis=-1, keepdims=True)
    return x * lax.rsqrt(ms + RMS_EPS) * g


def _sigmoid(x):
    return 1.0 / (1.0 + jnp.exp(-x))


EXP2_SCALE = SCALE * float(np.log2(np.e))


def _causal_flash(logits, values, n_full, last_biases, s_a, s_b, m_scr, acc_scr, n_full_even=False):
    def update(s_ref, n, bias=None):
        s = s_ref[...] if bias is None else s_ref[...] + bias
        m = m_scr[...]
        m_new = jnp.maximum(m, jnp.max(s, axis=1, keepdims=True))
        alpha = jnp.exp2((m - m_new) * EXP2_SCALE)
        p = jnp.exp2((s - m_new) * EXP2_SCALE)
        m_scr[...] = m_new
        acc_scr[...] = alpha * acc_scr[...] + _dot(p.astype(BF16), values(n))

    m_scr[...] = jnp.full(m_scr.shape, NEG, F32)
    acc_scr[...] = jnp.zeros(acc_scr.shape, F32)
    s_a[...] = logits(0)

    def pair(k, _):
        s_b[...] = logits(2 * k + 1)
        update(s_a, 2 * k)
        s_a[...] = logits(2 * k + 2)
        update(s_b, 2 * k + 1)
        return 0

    lax.fori_loop(0, n_full // 2, pair, 0)

    def tail(first, biases):
        bufs = (s_a, s_b)
        for i, bias in enumerate(biases):
            if i + 1 < len(biases):
                bufs[(i + 1) % 2][...] = logits(first + i + 1)
            update(bufs[i % 2], first + i, bias)

    if n_full_even:
        tail(n_full, list(last_biases))
    else:
        @pl.when(n_full % 2 == 1)
        def _():
            tail(n_full - 1, [None] + list(last_biases))

        @pl.when(n_full % 2 == 0)
        def _():
            tail(n_full, list(last_biases))


def _alibi_cols(slope_raw, lane, first):
    hi = slope_raw.astype(BF16).astype(F32)
    lo = (slope_raw - hi).astype(BF16).astype(F32)
    is_hi = (lane == first) | (lane == first + 2)
    is_lo = (lane == first + 1) | (lane == first + 3)
    return jnp.where(is_hi, hi, jnp.where(is_lo, lo, 0.0))


def _key_features(t, block, n_lanes_onehot, first):
    pos = np.arange(t)
    f = np.zeros((t, LANES), np.float32)
    f[pos, (pos // block) % n_lanes_onehot] = 1.0
    f[:, first] = f[:, first + 1] = 256 * (pos // 256)
    f[:, first + 2] = f[:, first + 3] = pos % 256
    return jnp.asarray(f, dtype=BF16)


def _pick_topk(score, lane, k, sentinel, axis=1):
    sel = jnp.zeros(score.shape, jnp.bool_)
    g = score
    sentinel = float(sentinel)
    for _ in range(k):
        m = jnp.max(g, axis=axis, keepdims=True)
        idx = jnp.min(jnp.where(g == m, lane, sentinel), axis=axis, keepdims=True)
        idx = jnp.where(m > -jnp.inf, idx, sentinel)
        pick = lane == idx
        sel = sel | pick
        g = jnp.where(pick, -jnp.inf, g)
    return sel


def _inproj_kernel(x_ref, g_ref, w_ref, wg_ref, proj_ref, gate_ref, h_scr):
    @pl.when(pl.program_id(1) == 0)
    def _():
        hb = _rms(x_ref[...], g_ref[...]).astype(BF16)
        h_scr[...] = hb
        gate_ref[...] = _dot(hb, wg_ref[...])

    proj_ref[...] = _dot(h_scr[...], w_ref[...]).astype(BF16)


def _inproj(x2, g, w_main, w_gate):
    n = x2.shape[0]
    tm = min(1024, n)
    tn = 512
    return pl.pallas_call(
        _inproj_kernel,
        grid=(n // tm, D_PROJ // tn),
        in_specs=[
            pl.BlockSpec((tm, D_MODEL), lambda i, j: (i, 0)),
            pl.BlockSpec((1, D_MODEL), lambda i, j: (0, 0)),
            pl.BlockSpec((D_MODEL, tn), lambda i, j: (0, j)),
            pl.BlockSpec((D_MODEL, 2 * LANES), lambda i, j: (0, 0)),
        ],
        out_specs=[
            pl.BlockSpec((tm, tn), lambda i, j: (i, j)),
            pl.BlockSpec((tm, 2 * LANES), lambda i, j: (i, 0)),
        ],
        out_shape=[jax.ShapeDtypeStruct((n, D_PROJ), BF16), jax.ShapeDtypeStruct((n, 2 * LANES), F32)],
        scratch_shapes=[pltpu.VMEM((tm, D_MODEL), BF16)],
        compiler_params=_cparams(("parallel", "arbitrary")),
        name="inproj",
    )(x2, g, w_main, w_gate)


def _kmean_kernel(k_ref, o_ref):
    k = k_ref[0].astype(F32)
    o_ref[0] = jnp.mean(k.reshape(8, MOBA_BLOCK, D_MOBA), axis=1)


def _kmean(proj3):
    b, t, _ = proj3.shape
    nb = t // MOBA_BLOCK
    return pl.pallas_call(
        _kmean_kernel,
        grid=(b, nb // 8),
        in_specs=[pl.BlockSpec((1, 8 * MOBA_BLOCK, D_MOBA), lambda bi, j: (bi, j, COL_MK // D_MOBA))],
        out_specs=pl.BlockSpec((1, 8, D_MOBA), lambda bi, j: (bi, j, 0)),
        out_shape=jax.ShapeDtypeStruct((b, nb, D_MOBA), F32),
        compiler_params=_cparams(("parallel", "parallel")),
        name="moba_kmean",
    )(proj3)


MOBA_TQ = 4 * MOBA_BLOCK
MOBA_TK = 2 * MOBA_BLOCK
MOBA_ALIBI_LANE = 120


def _flash_scratch(rows, tk):
    return [pltpu.VMEM((rows, tk), F32), pltpu.VMEM((rows, tk), F32), pltpu.VMEM((rows, 1), F32),
            pltpu.VMEM((rows, 2 * HEAD_DIM), F32)]


def _moba_kernel(slopes_ref, q_ref, k_ref, v_ref, km_ref, fk_ref, causal_ref, o_ref, s_a, s_b, m_scr, acc_scr):
    h = pl.program_id(1)
    j = pl.program_id(2)
    tq, tk = MOBA_TQ, MOBA_TK
    q = q_ref[0]
    km_hi, km_lo = _split2(km_ref[0])
    gate = _dot_nt(km_hi, q) + _dot_nt(km_lo, q)
    blk = lax.broadcasted_iota(I32, (LANES, tq), 0)
    own = j * (tq // MOBA_BLOCK) + lax.broadcasted_iota(I32, (LANES, tq), 1) // MOBA_BLOCK
    sel = _pick_topk(jnp.where(blk < own, gate, -jnp.inf), blk.astype(F32), MOBA_TOPK, LANES, axis=0)
    sel = sel | (blk == own)
    lane = lax.broadcasted_iota(I32, (tq, LANES), 1)
    slope_raw = jnp.full((tq, LANES), slopes_ref[h] / SCALE, F32)
    extra = jnp.where(lane < MOBA_ALIBI_LANE, jnp.where(sel, 0.0, NEG).T,
                      _alibi_cols(slope_raw, lane, MOBA_ALIBI_LANE))
    qx = jnp.concatenate([q, extra.astype(BF16)], axis=1)
    ones = jnp.ones((tk, HEAD_DIM), BF16)

    def logits(n):
        start = pl.multiple_of(n * tk, tk)
        kx = jnp.concatenate([k_ref[0, pl.ds(start, tk), :], fk_ref[pl.ds(start, tk), :]], axis=1)
        return _dot_nt(qx, kx)

    def values(n):
        start = pl.multiple_of(n * tk, tk)
        return jnp.concatenate([v_ref[0, pl.ds(start, tk), :], ones], axis=1)

    per = tq // tk
    _causal_flash(logits, values, j * per, [causal_ref[i] for i in range(per)], s_a, s_b, m_scr, acc_scr,
                  n_full_even=(per % 2 == 0))
    o_ref[0] = acc_scr[:, :HEAD_DIM] / acc_scr[:, HEAD_DIM:HEAD_DIM + 1]


def _moba(proj3, kmean_pad, slopes):
    b, t, _ = proj3.shape
    assert t // MOBA_BLOCK <= MOBA_ALIBI_LANE and t % MOBA_TQ == 0
    qb, kb, vb = COL_MQ // HEAD_DIM, COL_MK // HEAD_DIM, COL_MV // HEAD_DIM
    fkey = _key_features(t, MOBA_BLOCK, MOBA_ALIBI_LANE, MOBA_ALIBI_LANE)
    per = MOBA_TQ // MOBA_TK
    key_pos = np.arange(per)[:, None, None] * MOBA_TK + np.arange(MOBA_TK)[None, None, :]
    causal = jnp.asarray(np.where(key_pos <= np.arange(MOBA_TQ)[None, :, None], 0.0, NEG), dtype=F32)
    return pl.pallas_call(
        _moba_kernel,
        grid_spec=pltpu.PrefetchScalarGridSpec(
            num_scalar_prefetch=1,
            grid=(b, N_HEADS_MOBA, t // MOBA_TQ),
            in_specs=[
                pl.BlockSpec((1, MOBA_TQ, HEAD_DIM), lambda bi, h, j, s: (bi, j, qb + h)),
                pl.BlockSpec((1, t, HEAD_DIM), lambda bi, h, j, s: (bi, 0, kb + h)),
                pl.BlockSpec((1, t, HEAD_DIM), lambda bi, h, j, s: (bi, 0, vb + h)),
                pl.BlockSpec((1, LANES, HEAD_DIM), lambda bi, h, j, s: (bi, 0, h)),
                pl.BlockSpec((t, LANES), lambda bi, h, j, s: (0, 0)),
                pl.BlockSpec((per, MOBA_TQ, MOBA_TK), lambda bi, h, j, s: (0, 0, 0)),
            ],
            out_specs=pl.BlockSpec((1, MOBA_TQ, HEAD_DIM), lambda bi, h, j, s: (bi, j, h)),
            scratch_shapes=_flash_scratch(MOBA_TQ, MOBA_TK),
        ),
        out_shape=jax.ShapeDtypeStruct((b, t, D_MOBA), F32),
        compiler_params=_cparams(("parallel", "parallel", "arbitrary")),
        name="moba_attn",
    )(slopes, proj3, proj3, proj3, kmean_pad, fkey, causal)


def _gelu_tanh(x):
    c = np.float32(np.sqrt(2.0 / np.pi))
    return x * (0.5 * (1.0 + jnp.tanh(c * (x + 0.044715 * (x * x * x)))))


def _cmp_kernel(x_ref, w1c_ref, w1_ref, pos_ref, w2_ref, o_ref):
    x = x_ref[0, 0, 0]
    nch = x.shape[0]
    ab = _dot(x, w1c_ref[0])
    first = ab[:, :HEAD_DIM]
    second = pltpu.roll(ab[:, HEAD_DIM:], shift=nch - 1, axis=0)
    p_hi, p_lo = _split2(pos_ref[0])
    w_hi, w_lo = _split2(w1_ref[0])
    posc = _dot(p_hi, w_hi) + _dot(p_hi, w_lo) + _dot(p_lo, w_hi)
    hid = _gelu_tanh(first + second + posc[0:1])
    o_ref[0, 0, 0] = _dot(hid.astype(BF16), w2_ref[0]).astype(BF16)


def _compress(xc, w1cat, w1, pos, w2):
    _, b, g, nch, _ = xc.shape
    hid = HEAD_DIM
    return pl.pallas_call(
        _cmp_kernel,
        grid=(2, b, g),
        in_specs=[
            pl.BlockSpec((1, 1, 1, nch, CMP_STRIDE * HEAD_DIM), lambda s, bi, gi: (s, bi, gi, 0, 0)),
            pl.BlockSpec((1, CMP_STRIDE * HEAD_DIM, 2 * hid), lambda s, bi, gi: (s, 0, 0)),
            pl.BlockSpec((1, CMP_LEN * HEAD_DIM, hid), lambda s, bi, gi: (s, 0, 0)),
            pl.BlockSpec((1, 8, CMP_LEN * HEAD_DIM), lambda s, bi, gi: (s, 0, 0)),
            pl.BlockSpec((1, hid, HEAD_DIM), lambda s, bi, gi: (s, 0, 0)),
        ],
        out_specs=pl.BlockSpec((1, 1, 1, nch, HEAD_DIM), lambda s, bi, gi: (s, bi, gi, 0, 0)),
        out_shape=jax.ShapeDtypeStruct((2, b, g, nch, HEAD_DIM), BF16),
        compiler_params=_cparams(("parallel", "parallel", "parallel")),
        name="nsa_compress",
    )(xc, w1cat, w1, pos, w2)


CMP_TQ = 256
NSA_HALF_BLOCKS = LANES // 2


def _nsa_cmp_kernel(n_slc, slopes_ref, q_ref, kc_ref, vct_ref, ovt_ref, oc_ref, sb_ref, oct_scr):
    g = pl.program_id(1)
    i = pl.program_id(2)
    tq = CMP_TQ
    kc = kc_ref[0, 0, 0]
    vct = vct_ref[0, 0]
    nch = kc.shape[0]
    t = i * tq + lax.broadcasted_iota(I32, (nch, tq), 1)
    cend = lax.broadcasted_iota(I32, (nch, tq), 0) * CMP_STRIDE + (CMP_LEN - 1)
    valid = cend <= t
    dist = (t - cend).astype(F32)
    psum = jnp.zeros((nch, tq), F32)
    for r in range(NSA_REP):
        slope = slopes_ref[g * NSA_REP + r]
        q = q_ref[0, :, r * HEAD_DIM:(r + 1) * HEAD_DIM]
        s = jnp.where(valid, _dot_nt(kc, q) * SCALE - slope * dist, NEG)
        m = jnp.max(s, axis=0, keepdims=True)
        m = jnp.where(m > 0.5 * NEG, m, 0.0)
        e = jnp.where(valid, jnp.exp(s - m), 0.0)
        p = e / jnp.maximum(jnp.sum(e, axis=0, keepdims=True), 1e-30)
        oct_scr[...] = _dot(vct, p.astype(BF16))
        oc_ref[0, :, r * HEAD_DIM:(r + 1) * HEAD_DIM] = oct_scr[...].T
        psum = psum + p
    p_hi = psum.astype(BF16)
    rem = psum - p_hi.astype(F32)
    p_mid = rem.astype(BF16)
    p_lo = (rem - p_mid.astype(F32)).astype(BF16)
    ovt = ovt_ref[...]
    imp = _dot(ovt, p_hi) + _dot(ovt, p_mid) + _dot(ovt, p_lo)
    blk = lax.broadcasted_iota(I32, (LANES, tq), 0)
    bt = (i * tq + lax.broadcasted_iota(I32, (LANES, tq), 1)) // SLC_BLOCK
    forced = (blk == 0) | (blk == bt) | (blk == bt - 1)
    pri = jnp.where(blk > bt, -jnp.inf, jnp.where(forced, jnp.inf, imp))
    sel = _pick_topk(pri, blk.astype(F32), n_slc, LANES, axis=0)
    bias = jnp.where(sel, 0.0, NEG).T
    lane = lax.broadcasted_iota(I32, (tq, LANES), 1)
    low = lane < NSA_HALF_BLOCKS
    sb_ref[0, 0, 0] = jnp.where(low, bias, 0.0).astype(BF16)
    sb_ref[0, 0, 1] = jnp.where(low, pltpu.roll(bias, NSA_HALF_BLOCKS, axis=1), 0.0).astype(BF16)


def _nsa_cmp(proj3, kvc, overlap, slopes):
    b, t, _ = proj3.shape
    nch = kvc.shape[3]
    n_slc = min(SLC_TOPK, t // SLC_BLOCK)
    qblk = COL_NQ // (NSA_REP * HEAD_DIM)
    gw = NSA_REP * HEAD_DIM
    return pl.pallas_call(
        functools.partial(_nsa_cmp_kernel, n_slc),
        grid_spec=pltpu.PrefetchScalarGridSpec(
            num_scalar_prefetch=1,
            grid=(b, NSA_KV_HEADS, t // CMP_TQ),
            in_specs=[
                pl.BlockSpec((1, CMP_TQ, gw), lambda bi, g, i, s: (bi, i, qblk + g)),
                pl.BlockSpec((1, 1, 1, nch, HEAD_DIM), lambda bi, g, i, s: (0, bi, g, 0, 0)),
                pl.BlockSpec((1, 1, HEAD_DIM, nch), lambda bi, g, i, s: (bi, g, 0, 0)),
                pl.BlockSpec((LANES, nch), lambda bi, g, i, s: (0, 0)),
            ],
            out_specs=[
                pl.BlockSpec((1, CMP_TQ, gw), lambda bi, g, i, s: (bi, i, g)),
                pl.BlockSpec((1, 1, 2, CMP_TQ, LANES), lambda bi, g, i, s: (bi, g, 0, i, 0)),
            ],
            scratch_shapes=[pltpu.VMEM((HEAD_DIM, CMP_TQ), F32)],
        ),
        out_shape=[jax.ShapeDtypeStruct((b, t, D_NSA), F32),
                   jax.ShapeDtypeStruct((b, NSA_KV_HEADS, 2, t, LANES), BF16)],
        compiler_params=_cparams(("parallel", "parallel", "arbitrary")),
        name="nsa_cmp_attn",
    )(slopes, proj3, kvc, jnp.swapaxes(kvc[1], 2, 3), overlap.T)


NSA_TQ = 256
SEL_TK = 512
WIN_KEYS = WINDOW + NSA_TQ
HALF_TILES = NSA_HALF_BLOCKS * SLC_BLOCK // SEL_TK


def _nsa_main_kernel(slopes_ref, q_ref, ks_ref, vs_ref, kw_ref, vw_ref, fk_ref, sb_ref, oc_ref, gt_ref, o_ref,
                     s_a, s_b, m_scr, acc_scr):
    g = pl.program_id(1)
    i = pl.program_id(2)
    tq, rows = NSA_TQ, NSA_REP * NSA_TQ
    t0 = i * tq
    tile4 = lambda a: jnp.concatenate([a] * NSA_REP, axis=0)
    qs = jnp.concatenate([q_ref[0, :, r * HEAD_DIM:(r + 1) * HEAD_DIM] for r in range(NSA_REP)], axis=0)
    lane = lax.broadcasted_iota(I32, (rows, LANES), 1)
    slope_raw = jnp.concatenate(
        [jnp.full((tq, LANES), slopes_ref[g * NSA_REP + r] / SCALE, F32) for r in range(NSA_REP)], axis=0)
    alibi = _alibi_cols(slope_raw, lane, NSA_HALF_BLOCKS).astype(BF16)
    qx_lo = jnp.concatenate([qs, tile4(sb_ref[0, 0, 0]) + alibi], axis=1)
    qx_hi = jnp.concatenate([qs, tile4(sb_ref[0, 0, 1]) + alibi], axis=1)
    qx_win = jnp.concatenate([qs, alibi], axis=1)
    ones = jnp.ones((WIN_KEYS, HEAD_DIM), BF16)

    def logits(n):
        start = pl.multiple_of(n * SEL_TK, SEL_TK)
        kx = jnp.concatenate([ks_ref[0, pl.ds(start, SEL_TK), :], fk_ref[pl.ds(start, SEL_TK), :]], axis=1)
        return _dot_nt(jnp.where(n < HALF_TILES, qx_lo, qx_hi), kx)

    def values(n):
        start = pl.multiple_of(n * SEL_TK, SEL_TK)
        return jnp.concatenate([vs_ref[0, pl.ds(start, SEL_TK), :], ones[:SEL_TK]], axis=1)

    last = t0 // SEL_TK
    rq = lax.broadcasted_iota(I32, (tq, SEL_TK), 0)
    ck = lax.broadcasted_iota(I32, (tq, SEL_TK), 1)
    causal = jnp.where(ck - rq <= t0 - last * SEL_TK, 0.0, NEG)
    _causal_flash(logits, values, last, [tile4(causal)], s_a, s_b, m_scr, acc_scr)
    acc_s = acc_scr[...]

    wstart = pl.multiple_of(jnp.maximum(t0 - WINDOW, 0), NSA_TQ)
    kxw = jnp.concatenate([kw_ref[0, pl.ds(wstart, WIN_KEYS), :], fk_ref[pl.ds(wstart, WIN_KEYS), :]], axis=1)
    vxw = jnp.concatenate([vw_ref[0, pl.ds(wstart, WIN_KEYS), :], ones], axis=1)
    rel = (lax.broadcasted_iota(I32, (tq, WIN_KEYS), 1) + (wstart - t0)
           - lax.broadcasted_iota(I32, (tq, WIN_KEYS), 0))
    wbias = jnp.where(rel <= 0, jnp.where(rel > -WINDOW, 0.0, NEG), NEG)
    s = _dot_nt(qx_win, kxw) + tile4(wbias)
    p = jnp.exp2((s - jnp.max(s, axis=1, keepdims=True)) * EXP2_SCALE)
    acc_w = _dot(p.astype(BF16), vxw)

    o_s = acc_s[:, :HEAD_DIM] / acc_s[:, HEAD_DIM:HEAD_DIM + 1]
    o_w = acc_w[:, :HEAD_DIM] / acc_w[:, HEAD_DIM:HEAD_DIM + 1]
    gt = _sigmoid(gt_ref[0])
    for r in range(NSA_REP):
        dsl = slice(r * HEAD_DIM, (r + 1) * HEAD_DIM)
        rsl = slice(r * tq, (r + 1) * tq)
        o_ref[0, :, dsl] = (gt[:, 3 * r:3 * r + 1] * oc_ref[0, :, dsl]
                            + gt[:, 3 * r + 1:3 * r + 2] * o_s[rsl]
                            + gt[:, 3 * r + 2:3 * r + 3] * o_w[rsl])


def _nsa_main(proj3, selbias, o_c, gates3, slopes):
    b, t, _ = proj3.shape
    gw = NSA_REP * HEAD_DIM
    qblk = COL_NQ // gw
    cb = lambda col: col // HEAD_DIM
    kv_spec = lambda col: pl.BlockSpec((1, t, HEAD_DIM), lambda bi, g, i, s: (bi, 0, cb(col) + g))
    assert t % SEL_TK == 0 and t >= WIN_KEYS and t // SLC_BLOCK <= 2 * NSA_HALF_BLOCKS
    fkey = _key_features(t, SLC_BLOCK, NSA_HALF_BLOCKS, NSA_HALF_BLOCKS)
    return pl.pallas_call(
        _nsa_main_kernel,
        grid_spec=pltpu.PrefetchScalarGridSpec(
            num_scalar_prefetch=1,
            grid=(b, NSA_KV_HEADS, t // NSA_TQ),
            in_specs=[
                pl.BlockSpec((1, NSA_TQ, gw), lambda bi, g, i, s: (bi, i, qblk + g)),
                kv_spec(COL_KS), kv_spec(COL_VS), kv_spec(COL_KW), kv_spec(COL_VW),
                pl.BlockSpec((t, LANES), lambda bi, g, i, s: (0, 0)),
                pl.BlockSpec((1, 1, 2, NSA_TQ, LANES), lambda bi, g, i, s: (bi, g, 0, i, 0)),
                pl.BlockSpec((1, NSA_TQ, gw), lambda bi, g, i, s: (bi, i, g)),
                pl.BlockSpec((1, NSA_TQ, LANES), lambda bi, g, i, s: (bi, i, g)),
            ],
            out_specs=pl.BlockSpec((1, NSA_TQ, gw), lambda bi, g, i, s: (bi, i, g)),
            scratch_shapes=_flash_scratch(NSA_REP * NSA_TQ, SEL_TK),
        ),
        out_shape=jax.ShapeDtypeStruct((b, t, D_NSA), F32),
        compiler_params=_cparams(("parallel", "parallel", "arbitrary")),
        name="nsa_main",
    )(slopes, proj3, proj3, proj3, proj3, proj3, fkey, selbias, o_c, gates3)


def _outproj_kernel(om_ref, on_ref, x_ref, gm_ref, gn_ref, wt_ref, wb_ref, o_ref):
    a = _rms(om_ref[...], gm_ref[...]).astype(BF16)
    c = _rms(on_ref[...], gn_ref[...]).astype(BF16)
    o_ref[...] = x_ref[...] + (_dot(a, wt_ref[...]) + _dot(c, wb_ref[...]))


def _outproj(o_m, o_n, x2, gm, gn, w_top, w_bot):
    n = x2.shape[0]
    tm = min(512, n)
    row = lambda w: pl.BlockSpec((tm, w), lambda i: (i, 0))
    const = lambda r, c: pl.BlockSpec((r, c), lambda i: (0, 0))
    return pl.pallas_call(
        _outproj_kernel,
        grid=(n // tm,),
        in_specs=[row(D_MOBA), row(D_NSA), row(D_MODEL), const(1, D_MOBA), const(1, D_NSA),
                  const(D_MOBA, D_MODEL), const(D_NSA, D_MODEL)],
        out_specs=row(D_MODEL),
        out_shape=jax.ShapeDtypeStruct((n, D_MODEL), F32),
        compiler_params=_cparams(("parallel",)),
        name="outproj",
    )(o_m, o_n, x2, gm, gn, w_top, w_bot)


ROUTER_TM = 512


def _router_kernel(x_ref, g_ref, wr_ref, bias_ref, tri_ref, h_ref, eidx_ref, wgt_ref, pos_ref, cnt_ref, carry):
    step = pl.program_id(0)
    tm = ROUTER_TM

    @pl.when(step == 0)
    def _():
        carry[...] = jnp.zeros_like(carry)

    h = _rms(x_ref[...], g_ref[...])
    h_ref[...] = h
    h_hi, h_lo = _split2(h)
    w_hi, w_lo = _split2(wr_ref[...])
    logits = _dot_nt(w_hi, h_hi) + _dot_nt(w_hi, h_lo) + _dot_nt(w_lo, h_hi)
    aff = _sigmoid(logits)
    biased = aff + bias_ref[...][:, 0:1]

    b3 = biased.reshape(N_GROUPS, GROUP_SIZE, tm)
    sub = lax.broadcasted_iota(I32, b3.shape, 1)
    m1 = jnp.max(b3, axis=1, keepdims=True)
    i1 = jnp.min(jnp.where(b3 == m1, sub, GROUP_SIZE), axis=1, keepdims=True)
    m2 = jnp.max(jnp.where(sub == i1, -jnp.inf, b3), axis=1, keepdims=True)
    gscore = (m1 + m2).reshape(N_GROUPS, tm)

    giota = lax.broadcasted_iota(I32, (N_GROUPS, tm), 0)
    gsel = jnp.zeros((N_GROUPS, tm), jnp.bool_)
    gs = gscore
    for _ in range(TOPK_GROUPS):
        m = jnp.max(gs, axis=0, keepdims=True)
        idx = jnp.min(jnp.where(gs == m, giota, N_GROUPS), axis=0, keepdims=True)
        pick = giota == idx
        gsel = gsel | pick
        gs = jnp.where(pick, -jnp.inf, gs)
    emask = jnp.broadcast_to(jnp.where(gsel, 1.0, 0.0)[:, None, :], (N_GROUPS, GROUP_SIZE, tm)).reshape(N_EXPERTS, tm)

    eiota = lax.broadcasted_iota(I32, (N_EXPERTS, tm), 0)
    cand = jnp.where(emask > 0.5, biased, -jnp.inf)
    multi = jnp.zeros((N_EXPERTS, tm), F32)
    idxs, wts = [], []
    for _ in range(TOP_K):
        m = jnp.max(cand, axis=0, keepdims=True)
        idx = jnp.min(jnp.where(cand == m, eiota, N_EXPERTS), axis=0, keepdims=True)
        pick = eiota == idx
        idxs.append(idx)
        wts.append(jnp.sum(jnp.where(pick, aff, 0.0), axis=0, keepdims=True))
        multi = jnp.where(pick, 1.0, multi)
        cand = jnp.where(pick, -jnp.inf, cand)
    wsum = wts[0]
    for w in wts[1:]:
        wsum = wsum + w

    base = carry[...][:, 0:1]
    cum = _dot(multi.astype(BF16), tri_ref[...]) + base
    poss = [jnp.sum(jnp.where(eiota == idx, cum, 0.0), axis=0, keepdims=True) for idx in idxs]
    total = base + jnp.sum(multi, axis=1, keepdims=True)
    carry[...] = jnp.broadcast_to(total, carry.shape)
    cnt_ref[...] = jnp.broadcast_to(total, cnt_ref.shape)
    eidx_ref[...] = jnp.concatenate(idxs, axis=0)
    wgt_ref[...] = jnp.concatenate([w / wsum * ROUTED_SCALE for w in wts], axis=0)
    pos_ref[...] = jnp.concatenate(poss, axis=0).astype(I32)


def _router(x1, g, wr_t, bias, tri):
    n = x1.shape[0]
    tm = ROUTER_TM
    const = lambda r, c: pl.BlockSpec((r, c), lambda i: (0, 0))
    tok = lambda: pl.BlockSpec((TOP_K, tm), lambda i: (0, i))
    return pl.pallas_call(
        _router_kernel,
        grid=(n // tm,),
        in_specs=[pl.BlockSpec((tm, D_MODEL), lambda i: (i, 0)), const(1, D_MODEL), const(N_EXPERTS, D_MODEL),
                  const(N_EXPERTS, LANES), const(tm, tm)],
        out_specs=[pl.BlockSpec((tm, D_MODEL), lambda i: (i, 0)), tok(), tok(), tok(), const(N_EXPERTS, LANES)],
        out_shape=[jax.ShapeDtypeStruct((n, D_MODEL), F32), jax.ShapeDtypeStruct((TOP_K, n), I32),
                   jax.ShapeDtypeStruct((TOP_K, n), F32), jax.ShapeDtypeStruct((TOP_K, n), I32),
                   jax.ShapeDtypeStruct((N_EXPERTS, LANES), F32)],
        scratch_shapes=[pltpu.VMEM((N_EXPERTS, LANES), F32)],
        compiler_params=_cparams(("arbitrary",)),
        name="router",
    )(x1, g, wr_t, bias, tri)


DISPATCH_TM = 256
ZERO_ROWS = EXPERT_TILE + 8


def _dispatch_kernel(pad_ref, tail_ref, dest_ref, h_ref, xs_hbm, zeros_vmem, sem, zsem):
    step = pl.program_id(0)
    tm = DISPATCH_TM

    @pl.when(step == 0)
    def _():
        zeros_vmem[...] = jnp.zeros_like(zeros_vmem)

        def fill(e, _):
            start = pl.multiple_of((pad_ref[e] // 8) * 8, 8)
            pltpu.make_async_copy(zeros_vmem, xs_hbm.at[pl.ds(start, ZERO_ROWS), :], zsem).start()
            return 0

        lax.fori_loop(0, N_EXPERTS, fill, 0)

        def drain(e, _):
            pltpu.make_async_copy(zeros_vmem, xs_hbm.at[pl.ds(0, ZERO_ROWS), :], zsem).wait()
            return 0

        lax.fori_loop(0, N_EXPERTS, drain, 0)

        tail_start = tail_ref[0]
        n_tail = (xs_hbm.shape[0] - tail_start) // EXPERT_TILE

        def fill_tail(i, _):
            start = pl.multiple_of(tail_start + i * EXPERT_TILE, EXPERT_TILE)
            pltpu.make_async_copy(zeros_vmem.at[pl.ds(0, EXPERT_TILE), :],
                                  xs_hbm.at[pl.ds(start, EXPERT_TILE), :], zsem).start()
            return 0

        lax.fori_loop(0, n_tail, fill_tail, 0)

        def drain_tail(i, _):
            pltpu.make_async_copy(zeros_vmem.at[pl.ds(0, EXPERT_TILE), :],
                                  xs_hbm.at[pl.ds(0, EXPERT_TILE), :], zsem).wait()
            return 0

        lax.fori_loop(0, n_tail, drain_tail, 0)

    def issue(j, _):
        for k in range(TOP_K):
            pltpu.make_async_copy(h_ref.at[pl.ds(j, 1), :],
                                  xs_hbm.at[pl.ds(dest_ref[j * TOP_K + k], 1), :], sem).start()
        return 0

    lax.fori_loop(0, tm, issue, 0)

    def drain(k, _):
        pltpu.make_async_copy(h_ref, xs_hbm.at[pl.ds(0, tm), :], sem).wait()
        return 0

    lax.fori_loop(0, TOP_K, drain, 0)


def _dispatch(h2, dest_flat, pad_start, tail_start, p_alloc):
    n = h2.shape[0]
    tm = min(DISPATCH_TM, n)
    return pl.pallas_call(
        _dispatch_kernel,
        grid_spec=pltpu.PrefetchScalarGridSpec(
            num_scalar_prefetch=2,
            grid=(n // tm,),
            in_specs=[pl.BlockSpec((tm * TOP_K,), lambda i, s, u: (i,), memory_space=pltpu.SMEM),
                      pl.BlockSpec((tm, D_MODEL), lambda i, s, u: (i, 0))],
            out_specs=pl.BlockSpec(memory_space=pl.ANY),
            scratch_shapes=[pltpu.VMEM((ZERO_ROWS, D_MODEL), F32), pltpu.SemaphoreType.DMA,
                            pltpu.SemaphoreType.DMA],
        ),
        out_shape=jax.ShapeDtypeStruct((p_alloc, D_MODEL), F32),
        compiler_params=_cparams(("arbitrary",)),
        name="moe_dispatch",
    )(pad_start, tail_start, dest_flat, h2)


def _moe_kernel(ce_ref, nu_ref, x_ref, wg_ref, wu_ref, wd_ref, y_ref, wg_b, wu_b, wd_b):
    c = pl.program_id(0)

    @pl.when(c < nu_ref[0])
    def _():
        @pl.when((c == 0) | (ce_ref[c] != ce_ref[jnp.maximum(c - 1, 0)]))
        def _():
            wg_b[...] = wg_ref[0].astype(BF16)
            wu_b[...] = wu_ref[0].astype(BF16)
            wd_b[...] = wd_ref[0].astype(BF16)

        x = x_ref[...].astype(BF16)
        gp = _dot(x, wg_b[...])
        up = _dot(x, wu_b[...])
        y_ref[...] = _dot((gp * _sigmoid(gp) * up).astype(BF16), wd_b[...])

    @pl.when(c >= nu_ref[0])
    def _():
        y_ref[...] = jnp.zeros_like(y_ref)


def _moe(xs, chunk_e, n_used, w_gate_e, w_up_e, w_down_e, n_chunks):
    c = EXPERT_TILE
    row = lambda i, ce, nu: (jnp.minimum(i, nu[0] - 1), 0)
    wsel = lambda i, ce, nu: (ce[jnp.minimum(i, nu[0] - 1)], 0, 0)
    return pl.pallas_call(
        _moe_kernel,
        grid_spec=pltpu.PrefetchScalarGridSpec(
            num_scalar_prefetch=2,
            grid=(n_chunks,),
            in_specs=[pl.BlockSpec((c, D_MODEL), row),
                      pl.BlockSpec((1, D_MODEL, D_EXPERT), wsel),
                      pl.BlockSpec((1, D_MODEL, D_EXPERT), wsel),
                      pl.BlockSpec((1, D_EXPERT, D_MODEL), wsel)],
            out_specs=pl.BlockSpec((c, D_MODEL), lambda i, ce, nu: (i, 0)),
            scratch_shapes=[pltpu.VMEM((D_MODEL, D_EXPERT), BF16), pltpu.VMEM((D_MODEL, D_EXPERT), BF16),
                            pltpu.VMEM((D_EXPERT, D_MODEL), BF16)],
        ),
        out_shape=jax.ShapeDtypeStruct((n_chunks * c, D_MODEL), F32),
        compiler_params=_cparams(("arbitrary",)),
        name="moe_experts",
    )(chunk_e, n_used, xs, w_gate_e, w_up_e, w_down_e)


COMBINE_TM = 64


def _combine_kernel(dest_ref, next_ref, w_ref, y_hbm, o_ref, rows, sems):
    step = pl.program_id(0)
    tm = COMBINE_TM
    slot = step % 2

    def gather(idx_ref, to_slot):
        def issue(j, _):
            for k in range(TOP_K):
                pltpu.make_async_copy(y_hbm.at[pl.ds(idx_ref[j * TOP_K + k], 1), :],
                                      rows.at[to_slot, pl.ds(k * tm + j, 1), :], sems.at[to_slot]).start()
            return 0

        lax.fori_loop(0, tm, issue, 0)

    @pl.when(step == 0)
    def _():
        gather(dest_ref, 0)

    @pl.when(step + 1 < pl.num_programs(0))
    def _():
        gather(next_ref, 1 - slot)

    pltpu.make_async_copy(y_hbm.at[pl.ds(0, tm * TOP_K), :], rows.at[slot], sems.at[slot]).wait()
    w = w_ref[...]
    acc = w[:, 0:1] * rows[slot, 0:tm, :]
    for k in range(1, TOP_K):
        acc = acc + w[:, k:k + 1] * rows[slot, k * tm:(k + 1) * tm, :]
    o_ref[...] = acc


def _combine(ys, dest_flat, w_tok):
    n = w_tok.shape[0]
    tm = COMBINE_TM
    steps = n // tm
    return pl.pallas_call(
        _combine_kernel,
        grid=(steps,),
        in_specs=[pl.BlockSpec((tm * TOP_K,), lambda i: (i,), memory_space=pltpu.SMEM),
                  pl.BlockSpec((tm * TOP_K,), lambda i: (jnp.minimum(i + 1, steps - 1),), memory_space=pltpu.SMEM),
                  pl.BlockSpec((tm, TOP_K), lambda i: (i, 0)),
                  pl.BlockSpec(memory_space=pl.ANY)],
        out_specs=pl.BlockSpec((tm, D_MODEL), lambda i: (i, 0)),
        out_shape=jax.ShapeDtypeStruct((n, D_MODEL), F32),
        scratch_shapes=[pltpu.VMEM((2, tm * TOP_K, D_MODEL), F32), pltpu.SemaphoreType.DMA((2,))],
        compiler_params=_cparams(("arbitrary",)),
        name="moe_combine",
    )(dest_flat, dest_flat, w_tok, ys)


def _final_kernel(last, x1_ref, rt_ref, h_ref, p_ref, wg_ref, wu_ref, wd_ref, gp_ref, wple_ref, wpg_ref, gf_ref, o_ref):
    hb = h_ref[...].astype(BF16)
    gp = _dot(hb, wg_ref[...])
    up = _dot(hb, wu_ref[...])
    shared = _dot((gp * _sigmoid(gp) * up).astype(BF16), wd_ref[...])
    x2 = x1_ref[...] + (rt_ref[...] + shared)
    gate = _sigmoid(_dot(_rms(x2, gp_ref[...]).astype(BF16), wpg_ref[...]))
    x3 = x2 + _dot(p_ref[...].astype(BF16), wple_ref[...]) * gate
    o_ref[...] = _rms(x3, gf_ref[...]) if last else x3


def _final(x1, routed, h2, p2, wgs, wus, wds, g_ple, w_ple, w_pg, g_final, last):
    n = x1.shape[0]
    tm = min(256, n)
    row = lambda w: pl.BlockSpec((tm, w), lambda i: (i, 0))
    const = lambda r, c: pl.BlockSpec((r, c), lambda i: (0, 0))
    return pl.pallas_call(
        functools.partial(_final_kernel, last),
        grid=(n // tm,),
        in_specs=[row(D_MODEL), row(D_MODEL), row(D_MODEL), row(PLE_DIM),
                  const(D_MODEL, D_EXPERT), const(D_MODEL, D_EXPERT), const(D_EXPERT, D_MODEL),
                  const(1, D_MODEL), const(PLE_DIM, D_MODEL), const(D_MODEL, D_MODEL), const(1, D_MODEL)],
        out_specs=row(D_MODEL),
        out_shape=jax.ShapeDtypeStruct((n, D_MODEL), F32),
        compiler_params=_cparams(("parallel",)),
        name="shared_ple_final",
    )(x1, routed, h2, p2, wgs, wus, wds, g_ple, w_ple, w_pg, g_final)


def _alibi_slopes(n):
    return jnp.asarray(2.0 ** (-8.0 * np.arange(1, n + 1) / n), dtype=F32)


def _layer(x, p_i, norm_mix_g, w_in, moba_out_g, nsa_out_g, cmp_pos_k, cmp_w1_k, cmp_w2_k,
           cmp_pos_v, cmp_w1_v, cmp_w2_v, w_out, norm_ffn_g, w_router, router_bias,
           w_gate_e, w_up_e, w_down_e, w_gate_s, w_up_s, w_down_s, norm_ple_g, w_ple, w_ple_gate,
           norm_final_g, last):
    b, t, _ = x.shape
    n = b * t
    slopes = _alibi_slopes(N_HEADS_MOBA + N_HEADS_NSA)
    x2 = x.reshape(n, D_MODEL)

    w_main = w_in[:, :D_PROJ].astype(BF16)
    wg = w_in[:, COL_GATE:].reshape(D_MODEL, NSA_KV_HEADS, NSA_REP * 3)
    w_gate = jnp.pad(wg, ((0, 0), (0, 0), (0, LANES - NSA_REP * 3))).reshape(D_MODEL, 2 * LANES).astype(BF16)
    proj, gates = _inproj(x2, norm_mix_g.reshape(1, D_MODEL), w_main, w_gate)
    proj3 = proj.reshape(b, t, D_PROJ)
    gates3 = gates.reshape(b, t, 2 * LANES)

    nb = t // MOBA_BLOCK
    kmean = _kmean(proj3)
    kmean_pad = jnp.pad(kmean, ((0, 0), (0, LANES - nb), (0, 0)))
    o_m = _moba(proj3, kmean_pad, slopes[0::2])

    nch = t // CMP_STRIDE
    kvc = jnp.stack([proj3[:, :, COL_KC:COL_KC + D_KV], proj3[:, :, COL_VC:COL_VC + D_KV]])
    kvc = kvc.reshape(2, b, t, NSA_KV_HEADS, HEAD_DIM).transpose(0, 1, 3, 2, 4)
    kvc = kvc.reshape(2, b, NSA_KV_HEADS, nch, CMP_STRIDE * HEAD_DIM)
    w1 = jnp.stack([cmp_w1_k, cmp_w1_v])
    half = CMP_STRIDE * HEAD_DIM
    w1cat = jnp.concatenate([w1[:, :half], w1[:, half:]], axis=2).astype(BF16)
    pos = jnp.stack([cmp_pos_k, cmp_pos_v]).reshape(2, 1, CMP_LEN * HEAD_DIM)
    pos = jnp.broadcast_to(pos, (2, 8, CMP_LEN * HEAD_DIM))
    w2 = jnp.stack([cmp_w2_k, cmp_w2_v]).astype(BF16)
    kv_cmp = _compress(kvc, w1cat, w1, pos, w2)
    cs = np.arange(nch)[:, None] * CMP_STRIDE
    ss = np.arange(LANES)[None, :] * SLC_BLOCK
    overlap = jnp.asarray(((cs + CMP_LEN - 1 >= ss) & (cs <= ss + SLC_BLOCK - 1)).astype(np.float32), dtype=BF16)
    o_c, selbias = _nsa_cmp(proj3, kv_cmp, overlap, slopes[1::2])
    o_n = _nsa_main(proj3, selbias, o_c, gates3, slopes[1::2])

    w_out_b = w_out.astype(BF16)
    x1 = _outproj(o_m.reshape(n, D_MOBA), o_n.reshape(n, D_NSA), x2, moba_out_g.reshape(1, D_MOBA),
                  nsa_out_g.reshape(1, D_NSA), w_out_b[:D_MOBA], w_out_b[D_MOBA:])

    tri = jnp.asarray(np.triu(np.ones((ROUTER_TM, ROUTER_TM), np.float32), k=1), dtype=BF16)
    bias = jnp.broadcast_to(router_bias.astype(F32)[:, None], (N_EXPERTS, LANES))
    h2, eidx_t, wgt_t, pos_t, cnt = _router(x1, norm_ffn_g.reshape(1, D_MODEL), w_router.T, bias, tri)
    counts = cnt[:, 0].astype(I32)
    c = EXPERT_TILE
    pcounts = (counts + c - 1) // c * c
    pend = jnp.cumsum(pcounts)
    pstart = pend - pcounts
    experts = jnp.arange(N_EXPERTS, dtype=I32)
    seg_start = jnp.sum(jnp.where(eidx_t[..., None] == experts, pstart, 0), axis=-1)
    dest_flat = (seg_start + pos_t).T.reshape(n * TOP_K)
    n_chunks = -(-(n * TOP_K + N_EXPERTS * (c - 1)) // c)
    n_used = (pend[-1] // c).astype(I32).reshape(1)
    chunk_row = jnp.arange(n_chunks, dtype=I32)[:, None] * c
    chunk_e = jnp.minimum(jnp.sum((pend[None, :] <= chunk_row).astype(I32), axis=1), N_EXPERTS - 1)

    xs = _dispatch(h2, dest_flat, (pstart + counts).astype(I32), pend[-1:].astype(I32), (n_chunks + 2) * c)
    ys = _moe(xs, chunk_e, n_used, w_gate_e, w_up_e, w_down_e, n_chunks)
    routed = _combine(ys, dest_flat, wgt_t.T)

    return _final(x1, routed, h2, p_i.reshape(n, PLE_DIM), w_gate_s.astype(BF16), w_up_s.astype(BF16),
                  w_down_s.astype(BF16), norm_ple_g.reshape(1, D_MODEL), w_ple.astype(BF16),
                  w_ple_gate.astype(BF16), norm_final_g.reshape(1, D_MODEL), last).reshape(b, t, D_MODEL)


def kernel(x, p, norm_mix_g, w_in, moba_out_g, nsa_out_g, cmp_pos_k, cmp_w1_k, cmp_w2_k, cmp_pos_v, cmp_w1_v, cmp_w2_v, w_out, norm_ffn_g, w_router, router_bias, w_gate_e, w_up_e, w_down_e, w_gate_s, w_up_s, w_down_s, norm_ple_g, w_ple, w_ple_gate, norm_final_g):
    per_layer = (norm_mix_g, w_in, moba_out_g, nsa_out_g, cmp_pos_k, cmp_w1_k, cmp_w2_k, cmp_pos_v, cmp_w1_v,
                 cmp_w2_v, w_out, norm_ffn_g, w_router, router_bias, w_gate_e, w_up_e, w_down_e,
                 w_gate_s, w_up_s, w_down_s, norm_ple_g, w_ple, w_ple_gate)
    depth = p.shape[0]
    for i in range(depth):
        x = _layer(x, p[i], *(a[i] for a in per_layer), norm_final_g, i == depth - 1)
    return x
```

```python
import functools

import numpy as np
import jax
import jax.numpy as jnp
from jax import lax
from jax.experimental import pallas as pl
from jax.experimental.pallas import tpu as pltpu

F32 = jnp.float32
BF16 = jnp.bfloat16
I32 = jnp.int32

D_MODEL = 2048
HEAD_DIM = 128
N_HEADS_MOBA = 8
N_HEADS_NSA = 8
NSA_KV_HEADS = 2
NSA_REP = N_HEADS_NSA // NSA_KV_HEADS
MOBA_BLOCK = 256
MOBA_TOPK = 3
CMP_LEN = 32
CMP_STRIDE = 16
SLC_BLOCK = 64
SLC_TOPK = 16
WINDOW = 512
N_EXPERTS = 64
N_GROUPS = 8
GROUP_SIZE = N_EXPERTS // N_GROUPS
TOPK_GROUPS = 4
TOP_K = 8
D_EXPERT = 512
ROUTED_SCALE = 2.5
PLE_DIM = 256
RMS_EPS = 1e-6
D_MOBA = N_HEADS_MOBA * HEAD_DIM
D_NSA = N_HEADS_NSA * HEAD_DIM
D_KV = NSA_KV_HEADS * HEAD_DIM
COL_MQ, COL_MK, COL_MV, COL_NQ = 0, D_MOBA, 2 * D_MOBA, 3 * D_MOBA
COL_KC = COL_NQ + D_NSA
COL_VC, COL_KS, COL_VS, COL_KW, COL_VW = (COL_KC + D_KV * i for i in range(1, 6))
D_PROJ = COL_VW + D_KV
COL_GATE = D_PROJ

LANES = 128
V7X_VMEM_LIMIT = 56 * 1024 * 1024
NEG = -1e30

EXPERT_TILE = 512
SCALE = HEAD_DIM ** -0.5
NT = (((1,), (1,)), ((), ()))


def _cparams(sem):
    return pltpu.CompilerParams(dimension_semantics=sem, vmem_limit_bytes=V7X_VMEM_LIMIT)


def _dot(a, b):
    return jnp.dot(a, b, preferred_element_type=F32)


def _dot_nt(a, b):
    return lax.dot_general(a, b, NT, preferred_element_type=F32)


def _split2(a):
    hi = a.astype(BF16)
    lo = (a - hi.astype(F32)).astype(BF16)
    return hi, lo


def _rms(x, g):
    ms = jnp.mean(x * x, axis=-1, keepdims=True)
    return x * lax.rsqrt(ms + RMS_EPS) * g


def _sigmoid(x):
    return 1.0 / (1.0 + jnp.exp(-x))


EXP2_SCALE = SCALE * float(np.log2(np.e))


def _causal_flash(logits, values, n_full, last_biases, s_a, s_b, m_scr, acc_scr, n_full_even=False):
    def update(s_ref, n, bias=None):
        s = s_ref[...] if bias is None else s_ref[...] + bias
        m = m_scr[...]
        m_new = jnp.maximum(m, jnp.max(s, axis=1, keepdims=True))
        alpha = jnp.exp2((m - m_new) * EXP2_SCALE)
        p = jnp.exp2((s - m_new) * EXP2_SCALE)
        m_scr[...] = m_new
        acc_scr[...] = alpha * acc_scr[...] + _dot(p.astype(BF16), values(n))

    m_scr[...] = jnp.full(m_scr.shape, NEG, F32)
    acc_scr[...] = jnp.zeros(acc_scr.shape, F32)
    s_a[...] = logits(0)

    def pair(k, _):
        s_b[...] = logits(2 * k + 1)
        update(s_a, 2 * k)
        s_a[...] = logits(2 * k + 2)
        update(s_b, 2 * k + 1)
        return 0

    lax.fori_loop(0, n_full // 2, pair, 0)

    def tail(first, biases):
        bufs = (s_a, s_b)
        for i, bias in enumerate(biases):
            if i + 1 < len(biases):
                bufs[(i + 1) % 2][...] = logits(first + i + 1)
            update(bufs[i % 2], first + i, bias)

    if n_full_even:
        tail(n_full, list(last_biases))
    else:
        @pl.when(n_full % 2 == 1)
        def _():
            tail(n_full - 1, [None] + list(last_biases))

        @pl.when(n_full % 2 == 0)
        def _():
            tail(n_full, list(last_biases))


def _alibi_cols(slope_raw, lane, first):
    hi = slope_raw.astype(BF16).astype(F32)
    lo = (slope_raw - hi).astype(BF16).astype(F32)
    is_hi = (lane == first) | (lane == first + 2)
    is_lo = (lane == first + 1) | (lane == first + 3)
    return jnp.where(is_hi, hi, jnp.where(is_lo, lo, 0.0))


def _key_features(t, block, n_lanes_onehot, first):
    pos = np.arange(t)
    f = np.zeros((t, LANES), np.float32)
    f[pos, (pos // block) % n_lanes_onehot] = 1.0
    f[:, first] = f[:, first + 1] = 256 * (pos // 256)
    f[:, first + 2] = f[:, first + 3] = pos % 256
    return jnp.asarray(f, dtype=BF16)


def _pick_topk(score, lane, k, sentinel, axis=1):
    sel = jnp.zeros(score.shape, jnp.bool_)
    g = score
    sentinel = float(sentinel)
    for _ in range(k):
        m = jnp.max(g, axis=axis, keepdims=True)
        idx = jnp.min(jnp.where(g == m, lane, sentinel), axis=axis, keepdims=True)
        idx = jnp.where(m > -jnp.inf, idx, sentinel)
        pick = lane == idx
        sel = sel | pick
        g = jnp.where(pick, -jnp.inf, g)
    return sel


def _inproj_kernel(x_ref, g_ref, w_ref, wg_ref, proj_ref, gate_ref, h_scr):
    @pl.when(pl.program_id(1) == 0)
    def _():
        hb = _rms(x_ref[...], g_ref[...]).astype(BF16)
        h_scr[...] = hb
        gate_ref[...] = _dot(hb, wg_ref[...])

    proj_ref[...] = _dot(h_scr[...], w_ref[...]).astype(BF16)


def _inproj(x2, g, w_main, w_gate):
    n = x2.shape[0]
    tm = min(1024, n)
    tn = 512
    return pl.pallas_call(
        _inproj_kernel,
        grid=(n // tm, D_PROJ // tn),
        in_specs=[
            pl.BlockSpec((tm, D_MODEL), lambda i, j: (i, 0)),
            pl.BlockSpec((1, D_MODEL), lambda i, j: (0, 0)),
            pl.BlockSpec((D_MODEL, tn), lambda i, j: (0, j)),
            pl.BlockSpec((D_MODEL, 2 * LANES), lambda i, j: (0, 0)),
        ],
        out_specs=[
            pl.BlockSpec((tm, tn), lambda i, j: (i, j)),
            pl.BlockSpec((tm, 2 * LANES), lambda i, j: (i, 0)),
        ],
        out_shape=[jax.ShapeDtypeStruct((n, D_PROJ), BF16), jax.ShapeDtypeStruct((n, 2 * LANES), F32)],
        scratch_shapes=[pltpu.VMEM((tm, D_MODEL), BF16)],
        compiler_params=_cparams(("parallel", "arbitrary")),
        name="inproj",
    )(x2, g, w_main, w_gate)


def _kmean_kernel(k_ref, o_ref):
    k = k_ref[0].astype(F32)
    o_ref[0] = jnp.mean(k.reshape(8, MOBA_BLOCK, D_MOBA), axis=1)


def _kmean(proj3):
    b, t, _ = proj3.shape
    nb = t // MOBA_BLOCK
    return pl.pallas_call(
        _kmean_kernel,
        grid=(b, nb // 8),
        in_specs=[pl.BlockSpec((1, 8 * MOBA_BLOCK, D_MOBA), lambda bi, j: (bi, j, COL_MK // D_MOBA))],
        out_specs=pl.BlockSpec((1, 8, D_MOBA), lambda bi, j: (bi, j, 0)),
        out_shape=jax.ShapeDtypeStruct((b, nb, D_MOBA), F32),
        compiler_params=_cparams(("parallel", "parallel")),
        name="moba_kmean",
    )(proj3)


MOBA_TQ = 4 * MOBA_BLOCK
MOBA_TK = 2 * MOBA_BLOCK
MOBA_ALIBI_LANE = 120


def _flash_scratch(rows, tk):
    return [pltpu.VMEM((rows, tk), F32), pltpu.VMEM((rows, tk), F32), pltpu.VMEM((rows, 1), F32),
            pltpu.VMEM((rows, 2 * HEAD_DIM), F32)]


def _moba_kernel(slopes_ref, q_ref, k_ref, v_ref, km_ref, fk_ref, causal_ref, o_ref, s_a, s_b, m_scr, acc_scr):
    h = pl.program_id(1)
    j = pl.program_id(2)
    tq, tk = MOBA_TQ, MOBA_TK
    q = q_ref[0]
    km_hi, km_lo = _split2(km_ref[0])
    gate = _dot_nt(km_hi, q) + _dot_nt(km_lo, q)
    blk = lax.broadcasted_iota(I32, (LANES, tq), 0)
    own = j * (tq // MOBA_BLOCK) + lax.broadcasted_iota(I32, (LANES, tq), 1) // MOBA_BLOCK
    sel = _pick_topk(jnp.where(blk < own, gate, -jnp.inf), blk.astype(F32), MOBA_TOPK, LANES, axis=0)
    sel = sel | (blk == own)
    lane = lax.broadcasted_iota(I32, (tq, LANES), 1)
    slope_raw = jnp.full((tq, LANES), slopes_ref[h] / SCALE, F32)
    extra = jnp.where(lane < MOBA_ALIBI_LANE, jnp.where(sel, 0.0, NEG).T,
                      _alibi_cols(slope_raw, lane, MOBA_ALIBI_LANE))
    qx = jnp.concatenate([q, extra.astype(BF16)], axis=1)
    ones = jnp.ones((tk, HEAD_DIM), BF16)

    def logits(n):
        start = pl.multiple_of(n * tk, tk)
        kx = jnp.concatenate([k_ref[0, pl.ds(start, tk), :], fk_ref[pl.ds(start, tk), :]], axis=1)
        return _dot_nt(qx, kx)

    def values(n):
        start = pl.multiple_of(n * tk, tk)
        return jnp.concatenate([v_ref[0, pl.ds(start, tk), :], ones], axis=1)

    per = tq // tk
    _causal_flash(logits, values, j * per, [causal_ref[i] for i in range(per)], s_a, s_b, m_scr, acc_scr,
                  n_full_even=(per % 2 == 0))
    o_ref[0] = acc_scr[:, :HEAD_DIM] / acc_scr[:, HEAD_DIM:HEAD_DIM + 1]


def _moba(proj3, kmean_pad, slopes):
    b, t, _ = proj3.shape
    assert t // MOBA_BLOCK <= MOBA_ALIBI_LANE and t % MOBA_TQ == 0
    qb, kb, vb = COL_MQ // HEAD_DIM, COL_MK // HEAD_DIM, COL_MV // HEAD_DIM
    fkey = _key_features(t, MOBA_BLOCK, MOBA_ALIBI_LANE, MOBA_ALIBI_LANE)
    per = MOBA_TQ // MOBA_TK
    key_pos = np.arange(per)[:, None, None] * MOBA_TK + np.arange(MOBA_TK)[None, None, :]
    causal = jnp.asarray(np.where(key_pos <= np.arange(MOBA_TQ)[None, :, None], 0.0, NEG), dtype=F32)
    return pl.pallas_call(
        _moba_kernel,
        grid_spec=pltpu.PrefetchScalarGridSpec(
            num_scalar_prefetch=1,
            grid=(b, N_HEADS_MOBA, t // MOBA_TQ),
            in_specs=[
                pl.BlockSpec((1, MOBA_TQ, HEAD_DIM), lambda bi, h, j, s: (bi, j, qb + h)),
                pl.BlockSpec((1, t, HEAD_DIM), lambda bi, h, j, s: (bi, 0, kb + h)),
                pl.BlockSpec((1, t, HEAD_DIM), lambda bi, h, j, s: (bi, 0, vb + h)),
                pl.BlockSpec((1, LANES, HEAD_DIM), lambda bi, h, j, s: (bi, 0, h)),
                pl.BlockSpec((t, LANES), lambda bi, h, j, s: (0, 0)),
                pl.BlockSpec((per, MOBA_TQ, MOBA_TK), lambda bi, h, j, s: (0, 0, 0)),
            ],
            out_specs=pl.BlockSpec((1, MOBA_TQ, HEAD_DIM), lambda bi, h, j, s: (bi, j, h)),
            scratch_shapes=_flash_scratch(MOBA_TQ, MOBA_TK),
        ),
        out_shape=jax.ShapeDtypeStruct((b, t, D_MOBA), F32),
        compiler_params=_cparams(("parallel", "parallel", "arbitrary")),
        name="moba_attn",
    )(slopes, proj3, proj3, proj3, kmean_pad, fkey, causal)


def _gelu_tanh(x):
    c = np.float32(np.sqrt(2.0 / np.pi))
    return x * (0.5 * (1.0 + jnp.tanh(c * (x + 0.044715 * (x * x * x)))))


def _cmp_kernel(x_ref, w1c_ref, w1_ref, pos_ref, w2_ref, o_ref):
    x = x_ref[0, 0, 0]
    nch = x.shape[0]
    ab = _dot(x, w1c_ref[0])
    first = ab[:, :HEAD_DIM]
    second = pltpu.roll(ab[:, HEAD_DIM:], shift=nch - 1, axis=0)
    p_hi, p_lo = _split2(pos_ref[0])
    w_hi, w_lo = _split2(w1_ref[0])
    posc = _dot(p_hi, w_hi) + _dot(p_hi, w_lo) + _dot(p_lo, w_hi)
    hid = _gelu_tanh(first + second + posc[0:1])
    o_ref[0, 0, 0] = _dot(hid.astype(BF16), w2_ref[0]).astype(BF16)


def _compress(xc, w1cat, w1, pos, w2):
    _, b, g, nch, _ = xc.shape
    hid = HEAD_DIM
    return pl.pallas_call(
        _cmp_kernel,
        grid=(2, b, g),
        in_specs=[
            pl.BlockSpec((1, 1, 1, nch, CMP_STRIDE * HEAD_DIM), lambda s, bi, gi: (s, bi, gi, 0, 0)),
            pl.BlockSpec((1, CMP_STRIDE * HEAD_DIM, 2 * hid), lambda s, bi, gi: (s, 0, 0)),
            pl.BlockSpec((1, CMP_LEN * HEAD_DIM, hid), lambda s, bi, gi: (s, 0, 0)),
            pl.BlockSpec((1, 8, CMP_LEN * HEAD_DIM), lambda s, bi, gi: (s, 0, 0)),
            pl.BlockSpec((1, hid, HEAD_DIM), lambda s, bi, gi: (s, 0, 0)),
        ],
        out_specs=pl.BlockSpec((1, 1, 1, nch, HEAD_DIM), lambda s, bi, gi: (s, bi, gi, 0, 0)),
        out_shape=jax.ShapeDtypeStruct((2, b, g, nch, HEAD_DIM), BF16),
        compiler_params=_cparams(("parallel", "parallel", "parallel")),
        name="nsa_compress",
    )(xc, w1cat, w1, pos, w2)


CMP_TQ = 256
NSA_HALF_BLOCKS = LANES // 2


def _nsa_cmp_kernel(n_slc, slopes_ref, q_ref, kc_ref, vct_ref, ovt_ref, oc_ref, sb_ref, oct_scr):
    g = pl.program_id(1)
    i = pl.program_id(2)
    tq = CMP_TQ
    kc = kc_ref[0, 0, 0]
    vct = vct_ref[0, 0]
    nch = kc.shape[0]
    t = i * tq + lax.broadcasted_iota(I32, (nch, tq), 1)
    cend = lax.broadcasted_iota(I32, (nch, tq), 0) * CMP_STRIDE + (CMP_LEN - 1)
    valid = cend <= t
    dist = (t - cend).astype(F32)
    psum = jnp.zeros((nch, tq), F32)
    for r in range(NSA_REP):
        slope = slopes_ref[g * NSA_REP + r]
        q = q_ref[0, :, r * HEAD_DIM:(r + 1) * HEAD_DIM]
        s = jnp.where(valid, _dot_nt(kc, q) * SCALE - slope * dist, NEG)
        m = jnp.max(s, axis=0, keepdims=True)
        m = jnp.where(m > 0.5 * NEG, m, 0.0)
        e = jnp.where(valid, jnp.exp(s - m), 0.0)
        p = e / jnp.maximum(jnp.sum(e, axis=0, keepdims=True), 1e-30)
        oct_scr[...] = _dot(vct, p.astype(BF16))
        oc_ref[0, :, r * HEAD_DIM:(r + 1) * HEAD_DIM] = oct_scr[...].T
        psum = psum + p
    p_hi = psum.astype(BF16)
    rem = psum - p_hi.astype(F32)
    p_mid = rem.astype(BF16)
    p_lo = (rem - p_mid.astype(F32)).astype(BF16)
    ovt = ovt_ref[...]
    imp = _dot(ovt, p_hi) + _dot(ovt, p_mid) + _dot(ovt, p_lo)
    blk = lax.broadcasted_iota(I32, (LANES, tq), 0)
    bt = (i * tq + lax.broadcasted_iota(I32, (LANES, tq), 1)) // SLC_BLOCK
    forced = (blk == 0) | (blk == bt) | (blk == bt - 1)
    pri = jnp.where(blk > bt, -jnp.inf, jnp.where(forced, jnp.inf, imp))
    sel = _pick_topk(pri, blk.astype(F32), n_slc, LANES, axis=0)
    bias = jnp.where(sel, 0.0, NEG).T
    lane = lax.broadcasted_iota(I32, (tq, LANES), 1)
    low = lane < NSA_HALF_BLOCKS
    sb_ref[0, 0, 0] = jnp.where(low, bias, 0.0).astype(BF16)
    sb_ref[0, 0, 1] = jnp.where(low, pltpu.roll(bias, NSA_HALF_BLOCKS, axis=1), 0.0).astype(BF16)


def _nsa_cmp(proj3, kvc, overlap, slopes):
    b, t, _ = proj3.shape
    nch = kvc.shape[3]
    n_slc = min(SLC_TOPK, t // SLC_BLOCK)
    qblk = COL_NQ // (NSA_REP * HEAD_DIM)
    gw = NSA_REP * HEAD_DIM
    return pl.pallas_call(
        functools.partial(_nsa_cmp_kernel, n_slc),
        grid_spec=pltpu.PrefetchScalarGridSpec(
            num_scalar_prefetch=1,
            grid=(b, NSA_KV_HEADS, t // CMP_TQ),
            in_specs=[
                pl.BlockSpec((1, CMP_TQ, gw), lambda bi, g, i, s: (bi, i, qblk + g)),
                pl.BlockSpec((1, 1, 1, nch, HEAD_DIM), lambda bi, g, i, s: (0, bi, g, 0, 0)),
                pl.BlockSpec((1, 1, HEAD_DIM, nch), lambda bi, g, i, s: (bi, g, 0, 0)),
                pl.BlockSpec((LANES, nch), lambda bi, g, i, s: (0, 0)),
            ],
            out_specs=[
                pl.BlockSpec((1, CMP_TQ, gw), lambda bi, g, i, s: (bi, i, g)),
                pl.BlockSpec((1, 1, 2, CMP_TQ, LANES), lambda bi, g, i, s: (bi, g, 0, i, 0)),
            ],
            scratch_shapes=[pltpu.VMEM((HEAD_DIM, CMP_TQ), F32)],
        ),
        out_shape=[jax.ShapeDtypeStruct((b, t, D_NSA), F32),
                   jax.ShapeDtypeStruct((b, NSA_KV_HEADS, 2, t, LANES), BF16)],
        compiler_params=_cparams(("parallel", "parallel", "arbitrary")),
        name="nsa_cmp_attn",
    )(slopes, proj3, kvc, jnp.swapaxes(kvc[1], 2, 3), overlap.T)


NSA_TQ = 256
SEL_TK = 512
WIN_KEYS = WINDOW + NSA_TQ
HALF_TILES = NSA_HALF_BLOCKS * SLC_BLOCK // SEL_TK


def _nsa_main_kernel(slopes_ref, q_ref, ks_ref, vs_ref, kw_ref, vw_ref, fk_ref, sb_ref, oc_ref, gt_ref, o_ref,
                     s_a, s_b, m_scr, acc_scr):
    g = pl.program_id(1)
    i = pl.program_id(2)
    tq, rows = NSA_TQ, NSA_REP * NSA_TQ
    t0 = i * tq
    tile4 = lambda a: jnp.concatenate([a] * NSA_REP, axis=0)
    qs = jnp.concatenate([q_ref[0, :, r * HEAD_DIM:(r + 1) * HEAD_DIM] for r in range(NSA_REP)], axis=0)
    lane = lax.broadcasted_iota(I32, (rows, LANES), 1)
    slope_raw = jnp.concatenate(
        [jnp.full((tq, LANES), slopes_ref[g * NSA_REP + r] / SCALE, F32) for r in range(NSA_REP)], axis=0)
    alibi = _alibi_cols(slope_raw, lane, NSA_HALF_BLOCKS).astype(BF16)
    qx_lo = jnp.concatenate([qs, tile4(sb_ref[0, 0, 0]) + alibi], axis=1)
    qx_hi = jnp.concatenate([qs, tile4(sb_ref[0, 0, 1]) + alibi], axis=1)
    qx_win = jnp.concatenate([qs, alibi], axis=1)
    ones = jnp.ones((WIN_KEYS, HEAD_DIM), BF16)

    def logits(n):
        start = pl.multiple_of(n * SEL_TK, SEL_TK)
        kx = jnp.concatenate([ks_ref[0, pl.ds(start, SEL_TK), :], fk_ref[pl.ds(start, SEL_TK), :]], axis=1)
        return _dot_nt(jnp.where(n < HALF_TILES, qx_lo, qx_hi), kx)

    def values(n):
        start = pl.multiple_of(n * SEL_TK, SEL_TK)
        return jnp.concatenate([vs_ref[0, pl.ds(start, SEL_TK), :], ones[:SEL_TK]], axis=1)

    last = t0 // SEL_TK
    rq = lax.broadcasted_iota(I32, (tq, SEL_TK), 0)
    ck = lax.broadcasted_iota(I32, (tq, SEL_TK), 1)
    causal = jnp.where(ck - rq <= t0 - last * SEL_TK, 0.0, NEG)
    _causal_flash(logits, values, last, [tile4(causal)], s_a, s_b, m_scr, acc_scr)
    acc_s = acc_scr[...]

    wstart = pl.multiple_of(jnp.maximum(t0 - WINDOW, 0), NSA_TQ)
    kxw = jnp.concatenate([kw_ref[0, pl.ds(wstart, WIN_KEYS), :], fk_ref[pl.ds(wstart, WIN_KEYS), :]], axis=1)
    vxw = jnp.concatenate([vw_ref[0, pl.ds(wstart, WIN_KEYS), :], ones], axis=1)
    rel = (lax.broadcasted_iota(I32, (tq, WIN_KEYS), 1) + (wstart - t0)
           - lax.broadcasted_iota(I32, (tq, WIN_KEYS), 0))
    wbias = jnp.where(rel <= 0, jnp.where(rel > -WINDOW, 0.0, NEG), NEG)
    s = _dot_nt(qx_win, kxw) + tile4(wbias)
    p = jnp.exp2((s - jnp.max(s, axis=1, keepdims=True)) * EXP2_SCALE)
    acc_w = _dot(p.astype(BF16), vxw)

    o_s = acc_s[:, :HEAD_DIM] / acc_s[:, HEAD_DIM:HEAD_DIM + 1]
    o_w = acc_w[:, :HEAD_DIM] / acc_w[:, HEAD_DIM:HEAD_DIM + 1]
    gt = _sigmoid(gt_ref[0])
    for r in range(NSA_REP):
        dsl = slice(r * HEAD_DIM, (r + 1) * HEAD_DIM)
        rsl = slice(r * tq, (r + 1) * tq)
        o_ref[0, :, dsl] = (gt[:, 3 * r:3 * r + 1] * oc_ref[0, :, dsl]
                            + gt[:, 3 * r + 1:3 * r + 2] * o_s[rsl]
                            + gt[:, 3 * r + 2:3 * r + 3] * o_w[rsl])


def _nsa_main(proj3, selbias, o_c, gates3, slopes):
    b, t, _ = proj3.shape
    gw = NSA_REP * HEAD_DIM
    qblk = COL_NQ // gw
    cb = lambda col: col // HEAD_DIM
    kv_spec = lambda col: pl.BlockSpec((1, t, HEAD_DIM), lambda bi, g, i, s: (bi, 0, cb(col) + g))
    assert t % SEL_TK == 0 and t >= WIN_KEYS and t // SLC_BLOCK <= 2 * NSA_HALF_BLOCKS
    fkey = _key_features(t, SLC_BLOCK, NSA_HALF_BLOCKS, NSA_HALF_BLOCKS)
    return pl.pallas_call(
        _nsa_main_kernel,
        grid_spec=pltpu.PrefetchScalarGridSpec(
            num_scalar_prefetch=1,
            grid=(b, NSA_KV_HEADS, t // NSA_TQ),
            in_specs=[
                pl.BlockSpec((1, NSA_TQ, gw), lambda bi, g, i, s: (bi, i, qblk + g)),
                kv_spec(COL_KS), kv_spec(COL_VS), kv_spec(COL_KW), kv_spec(COL_VW),
                pl.BlockSpec((t, LANES), lambda bi, g, i, s: (0, 0)),
                pl.BlockSpec((1, 1, 2, NSA_TQ, LANES), lambda bi, g, i, s: (bi, g, 0, i, 0)),
                pl.BlockSpec((1, NSA_TQ, gw), lambda bi, g, i, s: (bi, i, g)),
                pl.BlockSpec((1, NSA_TQ, LANES), lambda bi, g, i, s: (bi, i, g)),
            ],
            out_specs=pl.BlockSpec((1, NSA_TQ, gw), lambda bi, g, i, s: (bi, i, g)),
            scratch_shapes=_flash_scratch(NSA_REP * NSA_TQ, SEL_TK),
        ),
        out_shape=jax.ShapeDtypeStruct((b, t, D_NSA), F32),
        compiler_params=_cparams(("parallel", "parallel", "arbitrary")),
        name="nsa_main",
    )(slopes, proj3, proj3, proj3, proj3, proj3, fkey, selbias, o_c, gates3)


def _outproj_kernel(om_ref, on_ref, x_ref, gm_ref, gn_ref, wt_ref, wb_ref, o_ref):
    a = _rms(om_ref[...], gm_ref[...]).astype(BF16)
    c = _rms(on_ref[...], gn_ref[...]).astype(BF16)
    o_ref[...] = x_ref[...] + (_dot(a, wt_ref[...]) + _dot(c, wb_ref[...]))


def _outproj(o_m, o_n, x2, gm, gn, w_top, w_bot):
    n = x2.shape[0]
    tm = min(512, n)
    row = lambda w: pl.BlockSpec((tm, w), lambda i: (i, 0))
    const = lambda r, c: pl.BlockSpec((r, c), lambda i: (0, 0))
    return pl.pallas_call(
        _outproj_kernel,
        grid=(n // tm,),
        in_specs=[row(D_MOBA), row(D_NSA), row(D_MODEL), const(1, D_MOBA), const(1, D_NSA),
                  const(D_MOBA, D_MODEL), const(D_NSA, D_MODEL)],
        out_specs=row(D_MODEL),
        out_shape=jax.ShapeDtypeStruct((n, D_MODEL), F32),
        compiler_params=_cparams(("parallel",)),
        name="outproj",
    )(o_m, o_n, x2, gm, gn, w_top, w_bot)


ROUTER_TM = 512


def _router_kernel(x_ref, g_ref, wr_ref, bias_ref, tri_ref, h_ref, eidx_ref, wgt_ref, pos_ref, cnt_ref, carry):
    step = pl.program_id(0)
    tm = ROUTER_TM

    @pl.when(step == 0)
    def _():
        carry[...] = jnp.zeros_like(carry)

    h = _rms(x_ref[...], g_ref[...])
    h_ref[...] = h
    h_hi, h_lo = _split2(h)
    w_hi, w_lo = _split2(wr_ref[...])
    logits = _dot_nt(w_hi, h_hi) + _dot_nt(w_hi, h_lo) + _dot_nt(w_lo, h_hi)
    aff = _sigmoid(logits)
    biased = aff + bias_ref[...][:, 0:1]

    b3 = biased.reshape(N_GROUPS, GROUP_SIZE, tm)
    sub = lax.broadcasted_iota(I32, b3.shape, 1)
    m1 = jnp.max(b3, axis=1, keepdims=True)
    i1 = jnp.min(jnp.where(b3 == m1, sub, GROUP_SIZE), axis=1, keepdims=True)
    m2 = jnp.max(jnp.where(sub == i1, -jnp.inf, b3), axis=1, keepdims=True)
    gscore = (m1 + m2).reshape(N_GROUPS, tm)

    giota = lax.broadcasted_iota(I32, (N_GROUPS, tm), 0)
    gsel = jnp.zeros((N_GROUPS, tm), jnp.bool_)
    gs = gscore
    for _ in range(TOPK_GROUPS):
        m = jnp.max(gs, axis=0, keepdims=True)
        idx = jnp.min(jnp.where(gs == m, giota, N_GROUPS), axis=0, keepdims=True)
        pick = giota == idx
        gsel = gsel | pick
        gs = jnp.where(pick, -jnp.inf, gs)
    emask = jnp.broadcast_to(jnp.where(gsel, 1.0, 0.0)[:, None, :], (N_GROUPS, GROUP_SIZE, tm)).reshape(N_EXPERTS, tm)

    eiota = lax.broadcasted_iota(I32, (N_EXPERTS, tm), 0)
    cand = jnp.where(emask > 0.5, biased, -jnp.inf)
    multi = jnp.zeros((N_EXPERTS, tm), F32)
    idxs, wts = [], []
    for _ in range(TOP_K):
        m = jnp.max(cand, axis=0, keepdims=True)
        idx = jnp.min(jnp.where(cand == m, eiota, N_EXPERTS), axis=0, keepdims=True)
        pick = eiota == idx
        idxs.append(idx)
        wts.append(jnp.sum(jnp.where(pick, aff, 0.0), axis=0, keepdims=True))
        multi = jnp.where(pick, 1.0, multi)
        cand = jnp.where(pick, -jnp.inf, cand)
    wsum = wts[0]
    for w in wts[1:]:
        wsum = wsum + w

    base = carry[...][:, 0:1]
    cum = _dot(multi.astype(BF16), tri_ref[...]) + base
    poss = [jnp.sum(jnp.where(eiota == idx, cum, 0.0), axis=0, keepdims=True) for idx in idxs]
    total = base + jnp.sum(multi, axis=1, keepdims=True)
    carry[...] = jnp.broadcast_to(total, carry.shape)
    cnt_ref[...] = jnp.broadcast_to(total, cnt_ref.shape)
    eidx_ref[...] = jnp.concatenate(idxs, axis=0)
    wgt_ref[...] = jnp.concatenate([w / wsum * ROUTED_SCALE for w in wts], axis=0)
    pos_ref[...] = jnp.concatenate(poss, axis=0).astype(I32)


def _router(x1, g, wr_t, bias, tri):
    n = x1.shape[0]
    tm = ROUTER_TM
    const = lambda r, c: pl.BlockSpec((r, c), lambda i: (0, 0))
    tok = lambda: pl.BlockSpec((TOP_K, tm), lambda i: (0, i))
    return pl.pallas_call(
        _router_kernel,
        grid=(n // tm,),
        in_specs=[pl.BlockSpec((tm, D_MODEL), lambda i: (i, 0)), const(1, D_MODEL), const(N_EXPERTS, D_MODEL),
                  const(N_EXPERTS, LANES), const(tm, tm)],
        out_specs=[pl.BlockSpec((tm, D_MODEL), lambda i: (i, 0)), tok(), tok(), tok(), const(N_EXPERTS, LANES)],
        out_shape=[jax.ShapeDtypeStruct((n, D_MODEL), F32), jax.ShapeDtypeStruct((TOP_K, n), I32),
                   jax.ShapeDtypeStruct((TOP_K, n), F32), jax.ShapeDtypeStruct((TOP_K, n), I32),
                   jax.ShapeDtypeStruct((N_EXPERTS, LANES), F32)],
        scratch_shapes=[pltpu.VMEM((N_EXPERTS, LANES), F32)],
        compiler_params=_cparams(("arbitrary",)),
        name="router",
    )(x1, g, wr_t, bias, tri)


DISPATCH_TM = 256
ZERO_ROWS = EXPERT_TILE + 8


def _dispatch_kernel(pad_ref, tail_ref, dest_ref, h_ref, xs_hbm, zeros_vmem, sem, zsem):
    step = pl.program_id(0)
    tm = DISPATCH_TM

    @pl.when(step == 0)
    def _():
        zeros_vmem[...] = jnp.zeros_like(zeros_vmem)

        def fill(e, _):
            start = pl.multiple_of((pad_ref[e] // 8) * 8, 8)
            pltpu.make_async_copy(zeros_vmem, xs_hbm.at[pl.ds(start, ZERO_ROWS), :], zsem).start()
            return 0

        lax.fori_loop(0, N_EXPERTS, fill, 0)

        def drain(e, _):
            pltpu.make_async_copy(zeros_vmem, xs_hbm.at[pl.ds(0, ZERO_ROWS), :], zsem).wait()
            return 0

        lax.fori_loop(0, N_EXPERTS, drain, 0)

        tail_start = tail_ref[0]
        n_tail = (xs_hbm.shape[0] - tail_start) // EXPERT_TILE

        def fill_tail(i, _):
            start = pl.multiple_of(tail_start + i * EXPERT_TILE, EXPERT_TILE)
            pltpu.make_async_copy(zeros_vmem.at[pl.ds(0, EXPERT_TILE), :],
                                  xs_hbm.at[pl.ds(start, EXPERT_TILE), :], zsem).start()
            return 0

        lax.fori_loop(0, n_tail, fill_tail, 0)

        def drain_tail(i, _):
            pltpu.make_async_copy(zeros_vmem.at[pl.ds(0, EXPERT_TILE), :],
                                  xs_hbm.at[pl.ds(0, EXPERT_TILE), :], zsem).wait()
            return 0

        lax.fori_loop(0, n_tail, drain_tail, 0)

    def issue(j, _):
        for k in range(TOP_K):
            pltpu.make_async_copy(h_ref.at[pl.ds(j, 1), :],
                                  xs_hbm.at[pl.ds(dest_ref[j * TOP_K + k], 1), :], sem).start(priority=k % 2)
        return 0

    lax.fori_loop(0, tm, issue, 0)

    def drain(k, _):
        pltpu.make_async_copy(h_ref, xs_hbm.at[pl.ds(0, tm), :], sem).wait()
        return 0

    lax.fori_loop(0, TOP_K, drain, 0)


def _dispatch(h2, dest_flat, pad_start, tail_start, p_alloc):
    n = h2.shape[0]
    tm = min(DISPATCH_TM, n)
    return pl.pallas_call(
        _dispatch_kernel,
        grid_spec=pltpu.PrefetchScalarGridSpec(
            num_scalar_prefetch=2,
            grid=(n // tm,),
            in_specs=[pl.BlockSpec((tm * TOP_K,), lambda i, s, u: (i,), memory_space=pltpu.SMEM),
                      pl.BlockSpec((tm, D_MODEL), lambda i, s, u: (i, 0))],
            out_specs=pl.BlockSpec(memory_space=pl.ANY),
            scratch_shapes=[pltpu.VMEM((ZERO_ROWS, D_MODEL), F32), pltpu.SemaphoreType.DMA,
                            pltpu.SemaphoreType.DMA],
        ),
        out_shape=jax.ShapeDtypeStruct((p_alloc, D_MODEL), F32),
        compiler_params=_cparams(("arbitrary",)),
        name="moe_dispatch",
    )(pad_start, tail_start, dest_flat, h2)


def _moe_kernel(ce_ref, nu_ref, x_ref, wg_ref, wu_ref, wd_ref, y_ref, wg_b, wu_b, wd_b):
    c = pl.program_id(0)

    @pl.when(c < nu_ref[0])
    def _():
        @pl.when((c == 0) | (ce_ref[c] != ce_ref[jnp.maximum(c - 1, 0)]))
        def _():
            wg_b[...] = wg_ref[0].astype(BF16)
            wu_b[...] = wu_ref[0].astype(BF16)
            wd_b[...] = wd_ref[0].astype(BF16)

        x = x_ref[...].astype(BF16)
        gp = _dot(x, wg_b[...])
        up = _dot(x, wu_b[...])
        y_ref[...] = _dot((gp * _sigmoid(gp) * up).astype(BF16), wd_b[...])

    @pl.when(c >= nu_ref[0])
    def _():
        y_ref[...] = jnp.zeros_like(y_ref)


def _moe(xs, chunk_e, n_used, w_gate_e, w_up_e, w_down_e, n_chunks):
    c = EXPERT_TILE
    row = lambda i, ce, nu: (jnp.minimum(i, nu[0] - 1), 0)
    wsel = lambda i, ce, nu: (ce[jnp.minimum(i, nu[0] - 1)], 0, 0)
    return pl.pallas_call(
        _moe_kernel,
        grid_spec=pltpu.PrefetchScalarGridSpec(
            num_scalar_prefetch=2,
            grid=(n_chunks,),
            in_specs=[pl.BlockSpec((c, D_MODEL), row),
                      pl.BlockSpec((1, D_MODEL, D_EXPERT), wsel),
                      pl.BlockSpec((1, D_MODEL, D_EXPERT), wsel),
                      pl.BlockSpec((1, D_EXPERT, D_MODEL), wsel)],
            out_specs=pl.BlockSpec((c, D_MODEL), lambda i, ce, nu: (i, 0)),
            scratch_shapes=[pltpu.VMEM((D_MODEL, D_EXPERT), BF16), pltpu.VMEM((D_MODEL, D_EXPERT), BF16),
                            pltpu.VMEM((D_EXPERT, D_MODEL), BF16)],
        ),
        out_shape=jax.ShapeDtypeStruct((n_chunks * c, D_MODEL), F32),
        compiler_params=_cparams(("arbitrary",)),
        name="moe_experts",
    )(chunk_e, n_used, xs, w_gate_e, w_up_e, w_down_e)


COMBINE_TM = 64


def _combine_kernel(dest_ref, next_ref, w_ref, y_hbm, o_ref, rows, sems):
    step = pl.program_id(0)
    tm = COMBINE_TM
    slot = step % 2

    def gather(idx_ref, to_slot):
        def issue(j, _):
            for k in range(TOP_K):
                pltpu.make_async_copy(y_hbm.at[pl.ds(idx_ref[j * TOP_K + k], 1), :],
                                      rows.at[to_slot, pl.ds(k * tm + j, 1), :], sems.at[to_slot]).start(priority=k % 2)
            return 0

        lax.fori_loop(0, tm, issue, 0)

    @pl.when(step == 0)
    def _():
        gather(dest_ref, 0)

    @pl.when(step + 1 < pl.num_programs(0))
    def _():
        gather(next_ref, 1 - slot)

    pltpu.make_async_copy(y_hbm.at[pl.ds(0, tm * TOP_K), :], rows.at[slot], sems.at[slot]).wait()
    w = w_ref[...]
    acc = w[:, 0:1] * rows[slot, 0:tm, :]
    for k in range(1, TOP_K):
        acc = acc + w[:, k:k + 1] * rows[slot, k * tm:(k + 1) * tm, :]
    o_ref[...] = acc


def _combine(ys, dest_flat, w_tok):
    n = w_tok.shape[0]
    tm = COMBINE_TM
    steps = n // tm
    return pl.pallas_call(
        _combine_kernel,
        grid=(steps,),
        in_specs=[pl.BlockSpec((tm * TOP_K,), lambda i: (i,), memory_space=pltpu.SMEM),
                  pl.BlockSpec((tm * TOP_K,), lambda i: (jnp.minimum(i + 1, steps - 1),), memory_space=pltpu.SMEM),
                  pl.BlockSpec((tm, TOP_K), lambda i: (i, 0)),
                  pl.BlockSpec(memory_space=pl.ANY)],
        out_specs=pl.BlockSpec((tm, D_MODEL), lambda i: (i, 0)),
        out_shape=jax.ShapeDtypeStruct((n, D_MODEL), F32),
        scratch_shapes=[pltpu.VMEM((2, tm * TOP_K, D_MODEL), F32), pltpu.SemaphoreType.DMA((2,))],
        compiler_params=_cparams(("arbitrary",)),
        name="moe_combine",
    )(dest_flat, dest_flat, w_tok, ys)


def _final_kernel(last, x1_ref, rt_ref, h_ref, p_ref, wg_ref, wu_ref, wd_ref, gp_ref, wple_ref, wpg_ref, gf_ref, o_ref):
    hb = h_ref[...].astype(BF16)
    gp = _dot(hb, wg_ref[...])
    up = _dot(hb, wu_ref[...])
    shared = _dot((gp * _sigmoid(gp) * up).astype(BF16), wd_ref[...])
    x2 = x1_ref[...] + (rt_ref[...] + shared)
    gate = _sigmoid(_dot(_rms(x2, gp_ref[...]).astype(BF16), wpg_ref[...]))
    x3 = x2 + _dot(p_ref[...].astype(BF16), wple_ref[...]) * gate
    o_ref[...] = _rms(x3, gf_ref[...]) if last else x3


def _final(x1, routed, h2, p2, wgs, wus, wds, g_ple, w_ple, w_pg, g_final, last):
    n = x1.shape[0]
    tm = min(256, n)
    row = lambda w: pl.BlockSpec((tm, w), lambda i: (i, 0))
    const = lambda r, c: pl.BlockSpec((r, c), lambda i: (0, 0))
    return pl.pallas_call(
        functools.partial(_final_kernel, last),
        grid=(n // tm,),
        in_specs=[row(D_MODEL), row(D_MODEL), row(D_MODEL), row(PLE_DIM),
                  const(D_MODEL, D_EXPERT), const(D_MODEL, D_EXPERT), const(D_EXPERT, D_MODEL),
                  const(1, D_MODEL), const(PLE_DIM, D_MODEL), const(D_MODEL, D_MODEL), const(1, D_MODEL)],
        out_specs=row(D_MODEL),
        out_shape=jax.ShapeDtypeStruct((n, D_MODEL), F32),
        compiler_params=_cparams(("parallel",)),
        name="shared_ple_final",
    )(x1, routed, h2, p2, wgs, wus, wds, g_ple, w_ple, w_pg, g_final)


def _alibi_slopes(n):
    return jnp.asarray(2.0 ** (-8.0 * np.arange(1, n + 1) / n), dtype=F32)


def _layer(x, p_i, norm_mix_g, w_in, moba_out_g, nsa_out_g, cmp_pos_k, cmp_w1_k, cmp_w2_k,
           cmp_pos_v, cmp_w1_v, cmp_w2_v, w_out, norm_ffn_g, w_router, router_bias,
           w_gate_e, w_up_e, w_down_e, w_gate_s, w_up_s, w_down_s, norm_ple_g, w_ple, w_ple_gate,
           norm_final_g, last):
    b, t, _ = x.shape
    n = b * t
    slopes = _alibi_slopes(N_HEADS_MOBA + N_HEADS_NSA)
    x2 = x.reshape(n, D_MODEL)

    w_main = w_in[:, :D_PROJ].astype(BF16)
    wg = w_in[:, COL_GATE:].reshape(D_MODEL, NSA_KV_HEADS, NSA_REP * 3)
    w_gate = jnp.pad(wg, ((0, 0), (0, 0), (0, LANES - NSA_REP * 3))).reshape(D_MODEL, 2 * LANES).astype(BF16)
    proj, gates = _inproj(x2, norm_mix_g.reshape(1, D_MODEL), w_main, w_gate)
    proj3 = proj.reshape(b, t, D_PROJ)
    gates3 = gates.reshape(b, t, 2 * LANES)

    nb = t // MOBA_BLOCK
    kmean = _kmean(proj3)
    kmean_pad = jnp.pad(kmean, ((0, 0), (0, LANES - nb), (0, 0)))
    o_m = _moba(proj3, kmean_pad, slopes[0::2])

    nch = t // CMP_STRIDE
    kvc = jnp.stack([proj3[:, :, COL_KC:COL_KC + D_KV], proj3[:, :, COL_VC:COL_VC + D_KV]])
    kvc = kvc.reshape(2, b, t, NSA_KV_HEADS, HEAD_DIM).transpose(0, 1, 3, 2, 4)
    kvc = kvc.reshape(2, b, NSA_KV_HEADS, nch, CMP_STRIDE * HEAD_DIM)
    w1 = jnp.stack([cmp_w1_k, cmp_w1_v])
    half = CMP_STRIDE * HEAD_DIM
    w1cat = jnp.concatenate([w1[:, :half], w1[:, half:]], axis=2).astype(BF16)
    pos = jnp.stack([cmp_pos_k, cmp_pos_v]).reshape(2, 1, CMP_LEN * HEAD_DIM)
    pos = jnp.broadcast_to(pos, (2, 8, CMP_LEN * HEAD_DIM))
    w2 = jnp.stack([cmp_w2_k, cmp_w2_v]).astype(BF16)
    kv_cmp = _compress(kvc, w1cat, w1, pos, w2)
    cs = np.arange(nch)[:, None] * CMP_STRIDE
    ss = np.arange(LANES)[None, :] * SLC_BLOCK
    overlap = jnp.asarray(((cs + CMP_LEN - 1 >= ss) & (cs <= ss + SLC_BLOCK - 1)).astype(np.float32), dtype=BF16)
    o_c, selbias = _nsa_cmp(proj3, kv_cmp, overlap, slopes[1::2])
    o_n = _nsa_main(proj3, selbias, o_c, gates3, slopes[1::2])

    w_out_b = w_out.astype(BF16)
    x1 = _outproj(o_m.reshape(n, D_MOBA), o_n.reshape(n, D_NSA), x2, moba_out_g.reshape(1, D_MOBA),
                  nsa_out_g.reshape(1, D_NSA), w_out_b[:D_MOBA], w_out_b[D_MOBA:])

    tri = jnp.asarray(np.triu(np.ones((ROUTER_TM, ROUTER_TM), np.float32), k=1), dtype=BF16)
    bias = jnp.broadcast_to(router_bias.astype(F32)[:, None], (N_EXPERTS, LANES))
    h2, eidx_t, wgt_t, pos_t, cnt = _router(x1, norm_ffn_g.reshape(1, D_MODEL), w_router.T, bias, tri)
    counts = cnt[:, 0].astype(I32)
    c = EXPERT_TILE
    pcounts = (counts + c - 1) // c * c
    pend = jnp.cumsum(pcounts)
    pstart = pend - pcounts
    experts = jnp.arange(N_EXPERTS, dtype=I32)
    seg_start = jnp.sum(jnp.where(eidx_t[..., None] == experts, pstart, 0), axis=-1)
    dest_flat = (seg_start + pos_t).T.reshape(n * TOP_K)
    n_chunks = -(-(n * TOP_K + N_EXPERTS * (c - 1)) // c)
    n_used = (pend[-1] // c).astype(I32).reshape(1)
    chunk_row = jnp.arange(n_chunks, dtype=I32)[:, None] * c
    chunk_e = jnp.minimum(jnp.sum((pend[None, :] <= chunk_row).astype(I32), axis=1), N_EXPERTS - 1)

    xs = _dispatch(h2, dest_flat, (pstart + counts).astype(I32), pend[-1:].astype(I32), (n_chunks + 2) * c)
    ys = _moe(xs, chunk_e, n_used, w_gate_e, w_up_e, w_down_e, n_chunks)
    routed = _combine(ys, dest_flat, wgt_t.T)

    return _final(x1, routed, h2, p_i.reshape(n, PLE_DIM), w_gate_s.astype(BF16), w_up_s.astype(BF16),
                  w_down_s.astype(BF16), norm_ple_g.reshape(1, D_MODEL), w_ple.astype(BF16),
                  w_ple_gate.astype(BF16), norm_final_g.reshape(1, D_MODEL), last).reshape(b, t, D_MODEL)


def kernel(x, p, norm_mix_g, w_in, moba_out_g, nsa_out_g, cmp_pos_k, cmp_w1_k, cmp_w2_k, cmp_pos_v, cmp_w1_v, cmp_w2_v, w_out, norm_ffn_g, w_router, router_bias, w_gate_e, w_up_e, w_down_e, w_gate_s, w_up_s, w_down_s, norm_ple_g, w_ple, w_ple_gate, norm_final_g):
    per_layer = (norm_mix_g, w_in, moba_out_g, nsa_out_g, cmp_pos_k, cmp_w1_k, cmp_w2_k, cmp_pos_v, cmp_w1_v,
                 cmp_w2_v, w_out, norm_ffn_g, w_router, router_bias, w_gate_e, w_up_e, w_down_e,
                 w_gate_s, w_up_s, w_down_s, norm_ple_g, w_ple, w_ple_gate)
    depth = p.shape[0]
    for i in range(depth):
        x = _layer(x, p[i], *(a[i] for a in per_layer), norm_final_g, i == depth - 1)
    return x
```
